```python
import math
import jax, jax.numpy as jnp
from jax import lax
import numpy as np

D_MODEL = 1024
BATCH = 1
SEQ = 16384
DEPTH = 4

HEAD_DIM = 64
N_MIXERS = 4
MIX_WIDTH = D_MODEL
N_HEADS = MIX_WIDTH // HEAD_DIM
HEADS_PER_MIXER = N_HEADS // N_MIXERS
GROUP_WIDTH = HEADS_PER_MIXER * HEAD_DIM
D_FF = 4 * D_MODEL
PLE_DIM = 256
QUERY_BLOCK = 128
NSA_CMP_LEN = 32
NSA_CMP_STRIDE = 16
NSA_CMP_HIDDEN = 256
NSA_SEL_LEN = 64
NSA_N_SEL = 16
NSA_WINDOW = 512
NSA_KV = HEAD_DIM
DILATED_PATTERNS = ((128, 1), (512, 4), (2048, 16))
N_ALIBI_HEADS = 2 * HEADS_PER_MIXER
FORGET_BIAS_INIT = 3.0
RMS_EPS = 1e-6
SEL_FORCE = 1e9
SPLITS = (GROUP_WIDTH, NSA_KV, NSA_KV, NSA_KV, NSA_KV, NSA_KV, NSA_KV, 3 * HEADS_PER_MIXER,
          GROUP_WIDTH, GROUP_WIDTH, GROUP_WIDTH,
          GROUP_WIDTH, GROUP_WIDTH, GROUP_WIDTH, HEADS_PER_MIXER,
          GROUP_WIDTH, GROUP_WIDTH, GROUP_WIDTH)
N_IN = sum(SPLITS)
F32 = jnp.float32

kernel_name = 'hybrid_parallel_heads_nsa_dilated_fox_stickbreak'


def rmsnorm(x, g):
    xf = x.astype(F32)
    y = xf * lax.rsqrt(jnp.mean(xf * xf, axis=-1, keepdims=True) + RMS_EPS)
    return (y * g.astype(F32)).astype(x.dtype)


def alibi_slopes():
    return (2.0 ** (-8.0 * np.arange(1, N_ALIBI_HEADS + 1) / N_ALIBI_HEADS)).astype(np.float32)


def masked_softmax(s, mask):
    m = jnp.max(jnp.where(mask, s, -jnp.inf), axis=-1, keepdims=True)
    m = jnp.where(jnp.isfinite(m), m, 0.0)
    e = jnp.where(mask, jnp.exp(s - m), 0.0)
    den = jnp.sum(e, axis=-1, keepdims=True)
    return e / jnp.maximum(den, 1e-30)


def to_blocks(t):
    B, H, S = t.shape[:3]
    t = t.reshape((B, H, S // QUERY_BLOCK, QUERY_BLOCK) + t.shape[3:])
    return jnp.moveaxis(t, 2, 0)


def from_blocks(t):
    nq, B, H, Q, Dh = t.shape
    return jnp.moveaxis(t, 0, 2).reshape(B, H, nq * Q, Dh)


def block_positions(S):
    return jnp.arange(S, dtype=jnp.int32).reshape(S // QUERY_BLOCK, QUERY_BLOCK)


def banded_attention(q, k, v, back, dist_scale, slopes):
    B, G, R, L, Dh = q.shape
    bq = math.gcd(L, QUERY_BLOCK)
    nb = L // bq
    span = bq + back
    idx = np.arange(nb)[:, None] * bq + np.arange(span)[None, :]
    pad = ((0, 0), (0, 0), (back, 0), (0, 0))
    kb = jnp.pad(k, pad)[:, :, idx]
    vb = jnp.pad(v, pad)[:, :, idx]
    qb = q.reshape(B, G, R, nb, bq, Dh)
    s = jnp.einsum('bgrnqd,bgnkd->bgrnqk', qb, kb, preferred_element_type=F32) * (Dh ** -0.5)
    dist = np.arange(bq)[:, None] + back - np.arange(span)[None, :]
    valid = (dist >= 0) & (dist <= back) & ((idx - back)[:, None, :] >= 0)
    s = s - (slopes * dist_scale)[:, :, None, None, None] * dist.astype(np.float32)
    s = jnp.where(valid, s, -jnp.inf)
    lse = jax.nn.logsumexp(s, axis=-1)
    pr = jnp.exp(s - lse[..., None])
    out = jnp.einsum('bgrnqk,bgnkd->bgrnqd', pr.astype(v.dtype), vb, preferred_element_type=F32)
    return out.reshape(B, G, R, L, Dh).astype(q.dtype), lse.reshape(B, G, R, L)


def nsa_mixer(q, k_cmp, v_cmp, k_sel, v_sel, k_win, v_win, gate_logits, b_gate,
              pos_k, w1_k, w2_k, pos_v, w1_v, w2_v, slopes):
    B, H, S, Dh = q.shape
    scale = Dh ** -0.5
    n_cmp = (S - NSA_CMP_LEN) // NSA_CMP_STRIDE + 1
    cstart = np.arange(n_cmp) * NSA_CMP_STRIDE
    cidx = cstart[:, None] + np.arange(NSA_CMP_LEN)[None, :]
    cend = cstart + NSA_CMP_LEN - 1
    cmid = (cstart + (NSA_CMP_LEN - 1) / 2.0).astype(np.float32)

    def compress(t, pos, w1, w2):
        blocks = t[:, cidx] + pos
        flat = blocks.reshape(B, n_cmp, NSA_CMP_LEN * Dh)
        return jax.nn.gelu(flat @ w1) @ w2

    kc = compress(k_cmp, pos_k, w1_k, w2_k)
    vc = compress(v_cmp, pos_v, w1_v, w2_v)
    n_blk = S // NSA_SEL_LEN
    n_sel = min(NSA_N_SEL, n_blk)
    bstart = np.arange(n_blk) * NSA_SEL_LEN
    cover = ((cstart[:, None] < bstart[None, :] + NSA_SEL_LEN) & (cend[:, None] >= bstart[None, :])).astype(np.float32)
    bidx = jnp.arange(n_blk)
    sl = slopes[:, None, None]

    def block(args):
        qb, tq = args
        tf = tq.astype(F32)
        s = jnp.einsum('bhqd,bcd->bhqc', qb, kc, preferred_element_type=F32) * scale
        s = s - sl * (tf[:, None] - cmid[None, :])
        p_c = masked_softmax(s, cend[None, :] <= tq[:, None])
        o_c = jnp.einsum('bhqc,bcd->bhqd', p_c.astype(vc.dtype), vc, preferred_element_type=F32)
        imp = jnp.einsum('bhqc,cn->bqn', p_c, cover)
        cur = tq // NSA_SEL_LEN
        forced = (bidx[None, :] == 0) | (bidx[None, :] == cur[:, None]) | (bidx[None, :] == cur[:, None] - 1)
        elig = bstart[None, :] <= tq[:, None]
        score = jnp.where(forced, SEL_FORCE, jnp.where(elig, imp, -SEL_FORCE))
        _, sel = lax.top_k(score, n_sel)
        tok = (sel[..., None] * NSA_SEL_LEN + jnp.arange(NSA_SEL_LEN)).reshape(B, QUERY_BLOCK, n_sel * NSA_SEL_LEN)
        bi = jnp.arange(B)[:, None, None]
        kg = k_sel[bi, tok]
        vg = v_sel[bi, tok]
        dist = tq[None, :, None] - tok
        s2 = jnp.einsum('bhqd,bqkd->bhqk', qb, kg, preferred_element_type=F32) * scale
        s2 = s2 - sl * dist[:, None].astype(F32)
        p_s = masked_softmax(s2, (dist >= 0)[:, None])
        o_s = jnp.einsum('bhqk,bqkd->bhqd', p_s.astype(vg.dtype), vg, preferred_element_type=F32)
        return o_c, o_s

    o_c, o_s = lax.map(block, (to_blocks(q), block_positions(S)))
    o_w, _ = banded_attention(q[:, None], k_win[:, None], v_win[:, None], NSA_WINDOW - 1, 1, slopes[None, :])
    g = jax.nn.sigmoid((gate_logits + b_gate).astype(F32)).reshape(B, S, 3, H).transpose(2, 0, 3, 1)[..., None]
    out = g[0] * from_blocks(o_c) + g[1] * from_blocks(o_s) + g[2] * o_w[:, 0].astype(F32)
    return out.astype(q.dtype)


def dilated_mixer(q, k, v, slopes):
    B, H, S, Dh = q.shape
    outs, lses = [], []
    for window, d in DILATED_PATTERNS:
        L = S // d

        def sub(t):
            return t.reshape(B, H, L, d, Dh).transpose(0, 1, 3, 2, 4).reshape(B, H * d, L, Dh)

        sl = np.repeat(slopes, d).reshape(H * d, 1)
        o, lse = banded_attention(sub(q)[:, :, None], sub(k), sub(v), window // d, d, sl)
        outs.append(o.reshape(B, H, d, L, Dh).transpose(0, 1, 3, 2, 4).reshape(B, H, S, Dh).astype(F32))
        lses.append(lse.reshape(B, H, d, L).transpose(0, 1, 3, 2).reshape(B, H, S))
    w = jax.nn.softmax(jnp.stack(lses), axis=0)
    return jnp.sum(w[..., None] * jnp.stack(outs), axis=0).astype(q.dtype)


def forgetting_mixer(q, k, v, f_logits, b_f):
    B, H, S, Dh = q.shape
    scale = Dh ** -0.5
    logf = jax.nn.log_sigmoid((f_logits + b_f).astype(F32)).transpose(0, 2, 1)
    c = jnp.cumsum(logf, axis=-1)
    spos = jnp.arange(S, dtype=jnp.int32)

    def block(args):
        qb, cq, tq = args
        s = jnp.einsum('bhqd,bhkd->bhqk', qb, k, preferred_element_type=F32) * scale
        s = s + (cq[..., None] - c[:, :, None, :])
        s = jnp.where(spos[None, :] <= tq[:, None], s, -jnp.inf)
        pr = jax.nn.softmax(s, axis=-1)
        return jnp.einsum('bhqk,bhkd->bhqd', pr.astype(v.dtype), v, preferred_element_type=F32)

    o = lax.map(block, (to_blocks(q), to_blocks(c), block_positions(S)))
    return from_blocks(o).astype(q.dtype)


def stick_breaking_mixer(q, k, v):
    B, H, S, Dh = q.shape
    scale = Dh ** -0.5
    spos = jnp.arange(S, dtype=jnp.int32)

    def block(args):
        qb, tq = args
        z = jnp.einsum('bhqd,bhkd->bhqk', qb, k, preferred_element_type=F32) * scale
        strict = spos[None, :] < tq[:, None]
        log_keep = jnp.where(strict, jax.nn.log_sigmoid(-z), 0.0)
        later = lax.cumsum(log_keep, axis=3, reverse=True) - log_keep
        a = jnp.where(strict, jnp.exp(jax.nn.log_sigmoid(z) + later), 0.0)
        return jnp.einsum('bhqk,bhkd->bhqd', a.astype(v.dtype), v, preferred_element_type=F32)

    o = lax.map(block, (to_blocks(q), block_positions(S)))
    return from_blocks(o).astype(q.dtype)


def setup_inputs(seed: int = 0) -> dict:
    key = jax.random.key(seed)
    ks = jax.random.split(key, 24)

    def nrm(k, shape, scale):
        return jax.random.normal(k, shape, F32) * scale

    def gain(k, shape):
        return 1.0 + 0.1 * jax.random.normal(k, shape, F32)

    H = HEADS_PER_MIXER
    flat = NSA_CMP_LEN * HEAD_DIM
    return {
        'x': nrm(ks[0], (BATCH, SEQ, D_MODEL), 1.0),
        'p': nrm(ks[1], (DEPTH, BATCH, SEQ, PLE_DIM), 1.0),
        'g_mix': gain(ks[2], (DEPTH, D_MODEL)),
        'w_in': nrm(ks[3], (DEPTH, D_MODEL, N_IN), D_MODEL ** -0.5),
        'b_nsa_gate': nrm(ks[4], (DEPTH, 3 * H), 0.1),
        'b_forget': FORGET_BIAS_INIT + nrm(ks[5], (DEPTH, H), 0.5),
        'cmp_pos_k': nrm(ks[6], (DEPTH, NSA_CMP_LEN, HEAD_DIM), 0.1),
        'cmp_w1_k': nrm(ks[7], (DEPTH, flat, NSA_CMP_HIDDEN), flat ** -0.5),
        'cmp_w2_k': nrm(ks[8], (DEPTH, NSA_CMP_HIDDEN, HEAD_DIM), NSA_CMP_HIDDEN ** -0.5),
        'cmp_pos_v': nrm(ks[9], (DEPTH, NSA_CMP_LEN, HEAD_DIM), 0.1),
        'cmp_w1_v': nrm(ks[10], (DEPTH, flat, NSA_CMP_HIDDEN), flat ** -0.5),
        'cmp_w2_v': nrm(ks[11], (DEPTH, NSA_CMP_HIDDEN, HEAD_DIM), NSA_CMP_HIDDEN ** -0.5),
        'g_head': gain(ks[12], (DEPTH, MIX_WIDTH)),
        'w_out': nrm(ks[13], (DEPTH, MIX_WIDTH, D_MODEL), MIX_WIDTH ** -0.5),
        'g_mlp': gain(ks[14], (DEPTH, D_MODEL)),
        'w_up': nrm(ks[15], (DEPTH, D_MODEL, D_FF), D_MODEL ** -0.5),
        'w_down': nrm(ks[16], (DEPTH, D_FF, D_MODEL), D_FF ** -0.5),
        'g_ple': gain(ks[17], (DEPTH, D_MODEL)),
        'w_ple_gate': nrm(ks[18], (DEPTH, D_MODEL, D_MODEL), D_MODEL ** -0.5),
        'b_ple_gate': nrm(ks[19], (DEPTH, D_MODEL), 0.1),
        'w_ple_proj': nrm(ks[20], (DEPTH, PLE_DIM, D_MODEL), PLE_DIM ** -0.5),
        'g_final': gain(ks[21], (D_MODEL,)),
    }


def reference(x, p, g_mix, w_in, b_nsa_gate, b_forget, cmp_pos_k, cmp_w1_k, cmp_w2_k,
              cmp_pos_v, cmp_w1_v, cmp_w2_v, g_head, w_out, g_mlp, w_up, w_down,
              g_ple, w_ple_gate, b_ple_gate, w_ple_proj, g_final):
    B, S, _ = x.shape
    slopes = alibi_slopes()
    slopes_nsa = slopes[0::2]
    slopes_dil = slopes[1::2]
    offs = np.cumsum(SPLITS)[:-1].tolist()

    def heads(t):
        return t.reshape(B, S, HEADS_PER_MIXER, HEAD_DIM).transpose(0, 2, 1, 3)

    h = x
    for i in range(DEPTH):
        u = rmsnorm(h, g_mix[i])
        (q_a, k_cmp, v_cmp, k_sel, v_sel, k_win, v_win, gate_a,
         q_b, k_b, v_b, q_c, k_c, v_c, f_c, q_d, k_d, v_d) = jnp.split(u @ w_in[i], offs, axis=-1)
        o_a = nsa_mixer(heads(q_a), k_cmp, v_cmp, k_sel, v_sel, k_win, v_win, gate_a, b_nsa_gate[i],
                        cmp_pos_k[i], cmp_w1_k[i], cmp_w2_k[i], cmp_pos_v[i], cmp_w1_v[i], cmp_w2_v[i],
                        slopes_nsa)
        o_b = dilated_mixer(heads(q_b), heads(k_b), heads(v_b), slopes_dil)
        o_c = forgetting_mixer(heads(q_c), heads(k_c), heads(v_c), f_c, b_forget[i])
        o_d = stick_breaking_mixer(heads(q_d), heads(k_d), heads(v_d))
        o = jnp.concatenate([o_a, o_b, o_c, o_d], axis=1).astype(h.dtype).transpose(0, 2, 1, 3)
        o = rmsnorm(o, g_head[i].reshape(N_HEADS, HEAD_DIM)).reshape(B, S, MIX_WIDTH)
        h = h + o @ w_out[i]
        u = rmsnorm(h, g_mlp[i])
        h = h + jnp.square(jax.nn.relu(u @ w_up[i])) @ w_down[i]
        gate = jax.nn.sigmoid(rmsnorm(h, g_ple[i]) @ w_ple_gate[i] + b_ple_gate[i])
        h = h + (p[i] @ w_ple_proj[i]) * gate
    return rmsnorm(h, g_final)
```

```python
import functools
import math

import numpy as np
import jax
import jax.numpy as jnp
from jax import lax
from jax.experimental import pallas as pl
from jax.experimental.pallas import tpu as pltpu

F32 = jnp.float32
BF16 = jnp.bfloat16

HEAD_DIM = 64
HEADS_PER_MIXER = 4
GROUP_WIDTH = HEADS_PER_MIXER * HEAD_DIM
N_ALIBI_HEADS = 2 * HEADS_PER_MIXER
NSA_CMP_LEN = 32
NSA_CMP_STRIDE = 16
NSA_SEL_LEN = 64
NSA_N_SEL = 16
NSA_WINDOW = 512
DILATED_PATTERNS = ((128, 1), (512, 4), (2048, 16))
RMS_EPS = 1e-6
SEL_FORCE = 1e9
QK_SCALE = HEAD_DIM ** -0.5

N_MAIN = 2944
N_GATE = 128
COL_QA, COL_KCMP, COL_VCMP, COL_KSEL, COL_VSEL, COL_KWIN, COL_VWIN = 0, 256, 320, 384, 448, 512, 576
COL_QB, COL_KB, COL_VB = 640, 896, 1152
COL_QC, COL_KC, COL_VC = 1408, 1664, 1920
COL_QD, COL_KD, COL_VD = 2176, 2432, 2688
GATE_COL_NSA, GATE_COL_F = 0, 12

MASKED = -1e9
M_INIT = -1e8
VMEM_LIMIT = 56 * 1024 * 1024


def _alibi_slopes():
    return 2.0 ** (-8.0 * np.arange(1, N_ALIBI_HEADS + 1) / N_ALIBI_HEADS)


def _params(*sem):
    return pltpu.CompilerParams(dimension_semantics=sem, vmem_limit_bytes=VMEM_LIMIT)


def _resident(shape, index_map):
    return pl.BlockSpec(shape, index_map, pipeline_mode=pl.Buffered(1))


def _dot(a, b):
    return jnp.dot(a, b, preferred_element_type=F32)


def _split3(x):
    hi = x.astype(BF16)
    r = x - hi.astype(F32)
    mid = r.astype(BF16)
    lo = (r - mid.astype(F32)).astype(BF16)
    return hi, mid, lo


def _dot3(x, w01):
    hi, mid, lo = _split3(x)
    return _dot(hi, w01) + _dot(mid, w01) + _dot(lo, w01)


def _rms(x, g):
    return x * lax.rsqrt(jnp.mean(x * x, axis=-1, keepdims=True) + RMS_EPS) * g


def _softplus(z):
    return jnp.maximum(z, 0.0) + jnp.log1p(jnp.exp(-jnp.abs(z)))


def _inproj_kernel(h_ref, g_ref, w_ref, main_ref, gate_ref):
    u = _rms(h_ref[...], g_ref[...]).astype(BF16)
    acc = _dot(u, w_ref[...])
    main_ref[...] = acc[:, :N_MAIN].astype(main_ref.dtype)
    gate_ref[...] = acc[:, N_MAIN:]


def _inproj(h, g, w, tm):
    S, D = h.shape
    n = w.shape[1]
    return pl.pallas_call(
        _inproj_kernel,
        grid=(S // tm,),
        in_specs=[pl.BlockSpec((tm, D), lambda i: (i, 0)),
                  _resident((1, D), lambda i: (0, 0)),
                  _resident((D, n), lambda i: (0, 0))],
        out_specs=[pl.BlockSpec((tm, N_MAIN), lambda i: (i, 0)),
                   pl.BlockSpec((tm, N_GATE), lambda i: (i, 0))],
        out_shape=[jax.ShapeDtypeStruct((S, N_MAIN), BF16), jax.ShapeDtypeStruct((S, N_GATE), F32)],
        compiler_params=_params("parallel"),
        name="inproj",
    )(h, g, w)


def _compress_kernel(flat_ref, pos_ref, w1_ref, w2_ref, out_ref):
    x = (flat_ref[0].astype(F32) + pos_ref[0]).astype(BF16)
    hid = _dot(x, w1_ref[0])
    c = math.sqrt(2.0 / math.pi)
    hid = 0.5 * hid * (1.0 + jnp.tanh(c * (hid + 0.044715 * (hid * hid * hid))))
    out_ref[0] = _dot(hid.astype(BF16), w2_ref[0])


def _compress(flat, pos, w1, w2, tm):
    _, n, width = flat.shape
    hidden = w1.shape[2]
    return pl.pallas_call(
        _compress_kernel,
        grid=(2, n // tm),
        in_specs=[pl.BlockSpec((1, tm, width), lambda a, i: (a, i, 0)),
                  pl.BlockSpec((1, 1, width), lambda a, i: (a, 0, 0)),
                  pl.BlockSpec((1, width, hidden), lambda a, i: (a, 0, 0)),
                  pl.BlockSpec((1, hidden, HEAD_DIM), lambda a, i: (a, 0, 0))],
        out_specs=pl.BlockSpec((1, tm, HEAD_DIM), lambda a, i: (a, i, 0)),
        out_shape=jax.ShapeDtypeStruct((2, n, HEAD_DIM), F32),
        compiler_params=_params("parallel", "parallel"),
        name="nsa_compress",
    )(flat, pos, w1, w2)


def _nsa_cmp_kernel(q_ref, kcT_ref, vc_ref, cover_ref, oc_ref, selb_ref, *, tq, slopes):
    t0 = pl.program_id(0) * tq
    nc = kcT_ref.shape[1]
    nb = cover_ref.shape[1]
    tpos = t0 + lax.broadcasted_iota(jnp.int32, (tq, nc), 0)
    jidx = lax.broadcasted_iota(jnp.int32, (tq, nc), 1)
    mask = jidx * NSA_CMP_STRIDE + (NSA_CMP_LEN - 1) <= tpos
    dist = tpos.astype(F32) - (jidx.astype(F32) * NSA_CMP_STRIDE + (NSA_CMP_LEN - 1) / 2.0)
    kcT = kcT_ref[...]
    vc = vc_ref[...]
    psum = jnp.zeros((tq, nc), F32)
    for h in range(HEADS_PER_MIXER):
        s = _dot(q_ref[h], kcT) - slopes[h] * dist
        m = jnp.max(jnp.where(mask, s, -jnp.inf), axis=-1, keepdims=True)
        m = jnp.where(m > -jnp.inf, m, 0.0)
        e = jnp.where(mask, jnp.exp(s - m), 0.0)
        den = jnp.sum(e, axis=-1, keepdims=True)
        p = e / jnp.maximum(den, 1e-30)
        oc_ref[h] = _dot(p.astype(BF16), vc)
        psum = psum + p
    imp = _dot3(psum, cover_ref[...])
    bidx = lax.broadcasted_iota(jnp.int32, (tq, nb), 1)
    qpos = t0 + lax.broadcasted_iota(jnp.int32, (tq, nb), 0)
    cur = qpos // NSA_SEL_LEN
    forced = jnp.where(bidx == 0, 1.0, jnp.where(bidx == cur, 1.0, jnp.where(bidx == cur - 1, 1.0, 0.0)))
    elig = bidx * NSA_SEL_LEN <= qpos
    score = jnp.where(forced > 0.0, SEL_FORCE, jnp.where(elig, imp, -SEL_FORCE))
    bf = bidx.astype(F32)
    sel = jnp.zeros((tq, nb), F32)
    for _ in range(min(NSA_N_SEL, nb)):
        mx = jnp.max(score, axis=-1, keepdims=True)
        first = jnp.min(jnp.where(score == mx, bf, float(nb)), axis=-1, keepdims=True)
        hit = bf == first
        sel = jnp.where(hit, 1.0, sel)
        score = jnp.where(hit, -jnp.inf, score)
    selb_ref[...] = jnp.where(elig, jnp.where(sel > 0.0, 0.0, MASKED), MASKED).astype(selb_ref.dtype)


def _nsa_cmp(q, kcT, vc, cover, tq, slopes):
    H, S, Dh = q.shape
    nc = kcT.shape[1]
    nb = cover.shape[1]
    return pl.pallas_call(
        functools.partial(_nsa_cmp_kernel, tq=tq, slopes=slopes),
        grid=(S // tq,),
        in_specs=[pl.BlockSpec((H, tq, Dh), lambda i: (0, i, 0)),
                  _resident((Dh, nc), lambda i: (0, 0)),
                  _resident((nc, Dh), lambda i: (0, 0)),
                  _resident((nc, nb), lambda i: (0, 0))],
        out_specs=[pl.BlockSpec((H, tq, Dh), lambda i: (0, i, 0)),
                   pl.BlockSpec((tq, nb), lambda i: (i, 0))],
        out_shape=[jax.ShapeDtypeStruct((H, S, Dh), F32), jax.ShapeDtypeStruct((S, nb), BF16)],
        compiler_params=_params("parallel"),
        name="nsa_cmp_select",
    )(q, kcT, vc, cover)


def _nsa_sel_kernel(q_ref, kT_ref, v_ref, selb_ref, expand_ref, o_ref, *, tq, tk, slopes):
    H = HEADS_PER_MIXER
    t0 = pl.program_id(0) * tq
    rows = H * tq
    q = q_ref[...].reshape(rows, HEAD_DIM)
    selb = selb_ref[...]
    selb4 = jnp.concatenate([selb] * H, axis=0)
    slope = jnp.concatenate([jnp.full((tq, 1), slopes[h], F32) for h in range(H)], axis=0)
    r1 = t0 + lax.broadcasted_iota(jnp.int32, (tq, tk), 0)
    rowpos = jnp.concatenate([r1] * H, axis=0)
    col = lax.broadcasted_iota(jnp.int32, (rows, tk), 1)

    def step(j, carry):
        m, l, acc = carry
        ks = pl.multiple_of(j * tk, tk)
        s = _dot(q, kT_ref[:, pl.ds(ks, tk)]) + _dot(selb4, expand_ref[:, pl.ds(ks, tk)])
        dist = rowpos - (ks + col)
        s = s - slope * dist.astype(F32)
        s = jnp.where(dist >= 0, s, MASKED)
        m_new = jnp.maximum(m, jnp.max(s, axis=-1, keepdims=True))
        p = jnp.exp(s - m_new)
        alpha = jnp.exp(m - m_new)
        l = alpha * l + jnp.sum(p, axis=-1, keepdims=True)
        acc = alpha * acc + _dot(p.astype(BF16), v_ref[pl.ds(ks, tk), :])
        return m_new, l, acc

    init = (jnp.full((rows, 1), M_INIT, F32), jnp.zeros((rows, 1), F32), jnp.zeros((rows, HEAD_DIM), F32))
    n_tiles = (t0 + tq - 1) // tk + 1
    m, l, acc = lax.fori_loop(0, n_tiles, step, init)
    o_ref[...] = (acc / jnp.maximum(l, 1e-30)).reshape(H, tq, HEAD_DIM)


def _nsa_sel(q, kT, v, selb, expand, tq, tk, slopes):
    H, S, Dh = q.shape
    nb = selb.shape[1]
    return pl.pallas_call(
        functools.partial(_nsa_sel_kernel, tq=tq, tk=tk, slopes=slopes),
        grid=(S // tq,),
        in_specs=[pl.BlockSpec((H, tq, Dh), lambda i: (0, i, 0)),
                  _resident((Dh, S), lambda i: (0, 0)),
                  _resident((S, Dh), lambda i: (0, 0)),
                  pl.BlockSpec((tq, nb), lambda i: (i, 0)),
                  _resident((nb, S), lambda i: (0, 0))],
        out_specs=pl.BlockSpec((H, tq, Dh), lambda i: (0, i, 0)),
        out_shape=jax.ShapeDtypeStruct((H, S, Dh), F32),
        compiler_params=_params("parallel"),
        name="nsa_selected",
    )(q, kT, v, selb, expand)


def _band_kernel(slope_ref, q_ref, kT_ref, v_ref, o_ref, lse_ref, *, tq, back, back_pad):
    g = pl.program_id(0)
    q0 = pl.multiple_of(pl.program_id(1) * tq, tq)
    width = back_pad + tq
    s = _dot(q_ref[0], kT_ref[0, :, pl.ds(q0, width)])
    r = lax.broadcasted_iota(jnp.int32, (tq, width), 0)
    c = lax.broadcasted_iota(jnp.int32, (tq, width), 1)
    dist = r + back_pad - c
    valid = jnp.where(dist >= 0, jnp.where(dist <= back, jnp.where(q0 - back_pad + c >= 0, 1.0, 0.0), 0.0), 0.0)
    s = s - slope_ref[g] * dist.astype(F32)
    s = jnp.where(valid > 0.0, s, -jnp.inf)
    m = jnp.max(s, axis=-1, keepdims=True)
    e = jnp.exp(s - m)
    l = jnp.sum(e, axis=-1, keepdims=True)
    o_ref[0] = _dot(e.astype(BF16), v_ref[0, pl.ds(q0, width), :]) / l
    lse_ref[0] = m + jnp.log(l)


def _band(slopes, q, kT, v, tq, back, kv_group):
    G, L, Dh = q.shape
    back_pad = kT.shape[2] - L
    Lp = kT.shape[2]
    grid_spec = pltpu.PrefetchScalarGridSpec(
        num_scalar_prefetch=1,
        grid=(G, L // tq),
        in_specs=[pl.BlockSpec((1, tq, Dh), lambda g, i, sl: (g, i, 0)),
                  pl.BlockSpec((1, Dh, Lp), lambda g, i, sl: (g // kv_group, 0, 0)),
                  pl.BlockSpec((1, Lp, Dh), lambda g, i, sl: (g // kv_group, 0, 0))],
        out_specs=[pl.BlockSpec((1, tq, Dh), lambda g, i, sl: (g, i, 0)),
                   pl.BlockSpec((1, tq, 1), lambda g, i, sl: (g, i, 0))],
    )
    return pl.pallas_call(
        functools.partial(_band_kernel, tq=tq, back=back, back_pad=back_pad),
        grid_spec=grid_spec,
        out_shape=[jax.ShapeDtypeStruct((G, L, Dh), F32), jax.ShapeDtypeStruct((G, L, 1), F32)],
        compiler_params=_params("parallel", "parallel"),
        name="band_attention",
    )(slopes, q, kT, v)


def _nsa_combine_kernel(oc_ref, os_ref, ow_ref, gl_ref, b_ref, out_ref):
    g = jax.nn.sigmoid(gl_ref[...] + b_ref[...])
    out_ref[0] = g[0, 0] * oc_ref[0] + g[1, 0] * os_ref[0] + g[2, 0] * ow_ref[0]


def _nsa_combine(oc, osel, ow, gl, b, tq):
    H, S, Dh = oc.shape
    o_spec = pl.BlockSpec((1, tq, Dh), lambda h, i: (h, i, 0))
    return pl.pallas_call(
        _nsa_combine_kernel,
        grid=(H, S // tq),
        in_specs=[o_spec, o_spec, o_spec,
                  pl.BlockSpec((3, 1, tq, 1), lambda h, i: (0, h, i, 0)),
                  pl.BlockSpec((3, 1, 1, 1), lambda h, i: (0, h, 0, 0))],
        out_specs=o_spec,
        out_shape=jax.ShapeDtypeStruct((H, S, Dh), F32),
        compiler_params=_params("parallel", "parallel"),
        name="nsa_combine",
    )(oc, osel, ow, gl, b)


def _dil_combine_kernel(o1_ref, o2_ref, o3_ref, l1_ref, l2_ref, l3_ref, out_ref):
    l1, l2, l3 = l1_ref[0], l2_ref[0], l3_ref[0]
    m = jnp.maximum(l1, jnp.maximum(l2, l3))
    e1, e2, e3 = jnp.exp(l1 - m), jnp.exp(l2 - m), jnp.exp(l3 - m)
    den = e1 + e2 + e3
    out_ref[0] = (e1 / den) * o1_ref[0] + (e2 / den) * o2_ref[0] + (e3 / den) * o3_ref[0]


def _dil_combine(outs, lses, tq):
    H, S, Dh = outs[0].shape
    o_spec = pl.BlockSpec((1, tq, Dh), lambda h, i: (h, i, 0))
    l_spec = pl.BlockSpec((1, tq, 1), lambda h, i: (h, i, 0))
    return pl.pallas_call(
        _dil_combine_kernel,
        grid=(H, S // tq),
        in_specs=[o_spec] * 3 + [l_spec] * 3,
        out_specs=o_spec,
        out_shape=jax.ShapeDtypeStruct((H, S, Dh), F32),
        compiler_params=_params("parallel", "parallel"),
        name="dilated_combine",
    )(*outs, *lses)


def _logf_cumsum_kernel(f_ref, b_ref, tri_ref, c_ref, carry_ref):
    @pl.when(pl.program_id(0) == 0)
    def _():
        carry_ref[...] = jnp.zeros_like(carry_ref)

    x = f_ref[...] + b_ref[...]
    logf = jnp.minimum(x, 0.0) - jnp.log1p(jnp.exp(-jnp.abs(x)))
    c = _dot3(logf, tri_ref[...]) + carry_ref[:, :1]
    c_ref[...] = c
    carry_ref[...] = jnp.broadcast_to(c[:, -1:], carry_ref.shape)


def _logf_cumsum(f, b, tk):
    R, S = f.shape
    tri = (np.arange(tk)[:, None] <= np.arange(tk)[None, :]).astype(np.float32)
    return pl.pallas_call(
        _logf_cumsum_kernel,
        grid=(S // tk,),
        in_specs=[pl.BlockSpec((R, tk), lambda i: (0, i)),
                  pl.BlockSpec((R, 1), lambda i: (0, 0)),
                  _resident((tk, tk), lambda i: (0, 0))],
        out_specs=pl.BlockSpec((R, tk), lambda i: (0, i)),
        out_shape=jax.ShapeDtypeStruct((R, S), F32),
        scratch_shapes=[pltpu.VMEM((R, 128), F32)],
        compiler_params=_params("arbitrary"),
        name="logf_cumsum",
    )(f, b, jnp.asarray(tri, BF16))


def _fox_kernel(q_ref, kT_ref, v_ref, c_ref, o_ref, *, tq):
    tk = tq
    i = pl.program_id(1)
    q = q_ref[0]
    row = lax.broadcasted_iota(jnp.int32, (tq, tk), 0)
    col = lax.broadcasted_iota(jnp.int32, (tq, tk), 1)

    def tile(j, carry, diagonal):
        m, l, acc = carry
        ks = pl.multiple_of(j * tk, tk)
        s = _dot(q, kT_ref[0, :, pl.ds(ks, tk)]) - c_ref[0, :, pl.ds(ks, tk)]
        if diagonal:
            s = jnp.where(col <= row, s, -jnp.inf)
        m_new = jnp.maximum(m, jnp.max(s, axis=-1, keepdims=True))
        p = jnp.exp(s - m_new)
        alpha = jnp.exp(m - m_new)
        l = alpha * l + jnp.sum(p, axis=-1, keepdims=True)
        acc = alpha * acc + _dot(p.astype(BF16), v_ref[0, pl.ds(ks, tk), :])
        return m_new, l, acc

    init = (jnp.full((tq, 1), -jnp.inf, F32), jnp.zeros((tq, 1), F32), jnp.zeros((tq, HEAD_DIM), F32))
    carry = lax.fori_loop(0, i, lambda j, c: tile(j, c, False), init)
    m, l, acc = tile(i, carry, True)
    o_ref[0] = acc / l


def _fox(q, kT, v, c, tq):
    H, S, Dh = q.shape
    return pl.pallas_call(
        functools.partial(_fox_kernel, tq=tq),
        grid=(H, S // tq),
        in_specs=[pl.BlockSpec((1, tq, Dh), lambda h, i: (h, i, 0)),
                  pl.BlockSpec((1, Dh, S), lambda h, i: (h, 0, 0)),
                  pl.BlockSpec((1, S, Dh), lambda h, i: (h, 0, 0)),
                  pl.BlockSpec((1, 1, S), lambda h, i: (h, 0, 0))],
        out_specs=pl.BlockSpec((1, tq, Dh), lambda h, i: (h, i, 0)),
        out_shape=jax.ShapeDtypeStruct((H, S, Dh), F32),
        compiler_params=_params("parallel", "parallel"),
        name="forgetting_attention",
    )(q, kT, v, c)


def _sb_kernel(q_ref, kT_ref, v_ref, tri_ref, o_ref, *, tq):
    tk = tq
    i = pl.program_id(1)
    q = q_ref[0]
    tri = tri_ref[...]
    row = lax.broadcasted_iota(jnp.int32, (tq, tk), 0)
    col = lax.broadcasted_iota(jnp.int32, (tq, tk), 1)

    def tile(j, carry, diagonal):
        run, acc = carry
        ks = pl.multiple_of(j * tk, tk)
        z = _dot(q, kT_ref[0, :, pl.ds(ks, tk)])
        log_keep = -_softplus(z)
        log_beta = z + log_keep
        if diagonal:
            log_keep = jnp.where(col < row, log_keep, 0.0)
        later = run + _dot3(log_keep, tri)
        a = jnp.exp(log_beta + later)
        if diagonal:
            a = jnp.where(col < row, a, 0.0)
        acc = acc + _dot(a.astype(BF16), v_ref[0, pl.ds(ks, tk), :])
        run = run + jnp.sum(log_keep, axis=-1, keepdims=True)
        return run, acc

    init = (jnp.zeros((tq, 1), F32), jnp.zeros((tq, HEAD_DIM), F32))
    carry = tile(i, init, True)
    _, acc = lax.fori_loop(0, i, lambda jj, c: tile(i - 1 - jj, c, False), carry)
    o_ref[0] = acc


def _sb(q, kT, v, tq):
    H, S, Dh = q.shape
    tri = (np.arange(tq)[:, None] > np.arange(tq)[None, :]).astype(np.float32)
    return pl.pallas_call(
        functools.partial(_sb_kernel, tq=tq),
        grid=(H, S // tq),
        in_specs=[pl.BlockSpec((1, tq, Dh), lambda h, i: (h, i, 0)),
                  pl.BlockSpec((1, Dh, S), lambda h, i: (h, 0, 0)),
                  pl.BlockSpec((1, S, Dh), lambda h, i: (h, 0, 0)),
                  _resident((tq, tq), lambda h, i: (0, 0))],
        out_specs=pl.BlockSpec((1, tq, Dh), lambda h, i: (h, i, 0)),
        out_shape=jax.ShapeDtypeStruct((H, S, Dh), F32),
        compiler_params=_params("parallel", "parallel"),
        name="stick_breaking_attention",
    )(q, kT, v, jnp.asarray(tri, BF16))


def _post_kernel(h_ref, o_ref, p_ref, ghead_ref, gmean_ref, wout_ref, gmlp_ref, wup_ref, wdown_ref,
                 gple_ref, wgate_ref, bgate_ref, wproj_ref, out_ref, *, f_chunk):
    o = o_ref[...]
    o2 = o * o
    hi = o2.astype(BF16)
    lo = (o2 - hi.astype(F32)).astype(BF16)
    ms = _dot(hi, gmean_ref[...]) + _dot(lo, gmean_ref[...])
    on = o * lax.rsqrt(ms + RMS_EPS) * ghead_ref[...]
    h = h_ref[...] + _dot(on.astype(BF16), wout_ref[...])
    u = _rms(h, gmlp_ref[...]).astype(BF16)
    d_ff = wup_ref.shape[1]
    acc = jnp.zeros(h.shape, F32)
    for f in range(0, d_ff, f_chunk):
        hid = jnp.maximum(_dot(u, wup_ref[:, f:f + f_chunk]), 0.0)
        acc = acc + _dot((hid * hid).astype(BF16), wdown_ref[f:f + f_chunk, :])
    h = h + acc
    gate = jax.nn.sigmoid(_dot(_rms(h, gple_ref[...]).astype(BF16), wgate_ref[...]) + bgate_ref[...])
    out_ref[...] = h + _dot(p_ref[...].astype(BF16), wproj_ref[...]) * gate


def _post(h, o, p, ghead, gmean, wout, gmlp, wup, wdown, gple, wgate, bgate, wproj, tm):
    S, D = h.shape
    row = lambda i: (i, 0)
    fixed = lambda i: (0, 0)
    vec = _resident((1, D), fixed)
    return pl.pallas_call(
        functools.partial(_post_kernel, f_chunk=1024),
        grid=(S // tm,),
        in_specs=[pl.BlockSpec((tm, D), row), pl.BlockSpec((tm, D), row), pl.BlockSpec((tm, p.shape[1]), row),
                  vec, _resident(gmean.shape, fixed), _resident(wout.shape, fixed),
                  vec, _resident(wup.shape, fixed), _resident(wdown.shape, fixed),
                  vec, _resident(wgate.shape, fixed), vec, _resident(wproj.shape, fixed)],
        out_specs=pl.BlockSpec((tm, D), row),
        out_shape=jax.ShapeDtypeStruct((S, D), F32),
        compiler_params=_params("parallel"),
        name="outproj_mlp_ple",
    )(h, o, p, ghead, gmean, wout, gmlp, wup, wdown, gple, wgate, bgate, wproj)


def _final_norm_kernel(h_ref, g_ref, out_ref):
    out_ref[...] = _rms(h_ref[...], g_ref[...])


def _final_norm(h, g, tm):
    S, D = h.shape
    return pl.pallas_call(
        _final_norm_kernel,
        grid=(S // tm,),
        in_specs=[pl.BlockSpec((tm, D), lambda i: (i, 0)), pl.BlockSpec((1, D), lambda i: (0, 0))],
        out_specs=pl.BlockSpec((tm, D), lambda i: (i, 0)),
        out_shape=jax.ShapeDtypeStruct((S, D), F32),
        compiler_params=_params("parallel"),
        name="final_norm",
    )(h, g)


def _inproj_columns():
    splits = (256, 64, 64, 64, 64, 64, 64, 12, 256, 256, 256, 256, 256, 256, 4, 256, 256, 256)
    offs = np.concatenate([[0], np.cumsum(splits)])
    seg = lambda k: np.arange(offs[k], offs[k + 1])
    main = [0, 1, 2, 3, 4, 5, 6, 8, 9, 10, 11, 12, 13, 15, 16, 17]
    perm = np.concatenate([seg(k) for k in main] + [seg(7), seg(14)])
    scale = np.ones(perm.shape[0], np.float32)
    for start in (COL_QA, COL_QB, COL_QC, COL_QD):
        scale[start:start + GROUP_WIDTH] = QK_SCALE
    return perm, scale


def _heads(t):
    S = t.shape[0]
    return t.reshape(S, HEADS_PER_MIXER, HEAD_DIM).transpose(1, 0, 2)


def _heads_T(t):
    S = t.shape[0]
    return t.reshape(S, HEADS_PER_MIXER, HEAD_DIM).transpose(1, 2, 0)


def _nsa_mixer(main, gates, b_gate, pos_k, w1_k, w2_k, pos_v, w1_v, w2_v, slopes, consts):
    S = main.shape[0]
    q = _heads(main[:, COL_QA:COL_QA + GROUP_WIDTH])
    col = lambda c: main[:, c:c + HEAD_DIM]
    n16 = S // NSA_CMP_STRIDE

    def blocks(t):
        t16 = t.reshape(n16, NSA_CMP_STRIDE * HEAD_DIM)
        return jnp.concatenate([t16, jnp.roll(t16, -1, axis=0)], axis=1)

    flat = jnp.stack([blocks(col(COL_KCMP)), blocks(col(COL_VCMP))])
    pos = jnp.stack([pos_k.reshape(1, -1), pos_v.reshape(1, -1)])
    w1 = jnp.stack([w1_k, w1_v]).astype(BF16)
    w2 = jnp.stack([w2_k, w2_v]).astype(BF16)
    kvc = _compress(flat, pos, w1, w2, tm=min(256, n16))
    kcT = kvc[0].T.astype(BF16)
    vc = kvc[1].astype(BF16)
    tq_c = min(128, S)
    o_cmp, selb = _nsa_cmp(q, kcT, vc, consts["cover"], tq_c, slopes)
    tk_s = min(512, S)
    o_sel = _nsa_sel(q, col(COL_KSEL).T, col(COL_VSEL), selb, consts["expand"], tq_c, tk_s, slopes)
    back = NSA_WINDOW - 1
    back_pad = -(-back // 128) * 128
    kT_w = jnp.pad(col(COL_KWIN).T, ((0, 0), (back_pad, 0)))[None]
    v_w = jnp.pad(col(COL_VWIN), ((back_pad, 0), (0, 0)))[None]
    o_win, _ = _band(jnp.asarray(slopes, F32), q, kT_w, v_w, min(256, S), back, HEADS_PER_MIXER)
    gl = gates[:, GATE_COL_NSA:GATE_COL_NSA + 3 * HEADS_PER_MIXER]
    gl = gl.reshape(S, 3, HEADS_PER_MIXER).transpose(1, 2, 0)[..., None]
    return _nsa_combine(o_cmp, o_sel, o_win, gl, b_gate.reshape(3, HEADS_PER_MIXER, 1, 1), min(512, S))


def _dilated_mixer(main, slopes):
    S = main.shape[0]
    H, Dh = HEADS_PER_MIXER, HEAD_DIM
    outs, lses = [], []
    for window, d in DILATED_PATTERNS:
        L = S // d
        back = window // d
        back_pad = -(-back // 128) * 128

        def sub(t):
            return t.reshape(L, d, H, Dh).transpose(2, 1, 0, 3).reshape(H * d, L, Dh)

        q = sub(main[:, COL_QB:COL_QB + GROUP_WIDTH])
        k = sub(main[:, COL_KB:COL_KB + GROUP_WIDTH])
        v = sub(main[:, COL_VB:COL_VB + GROUP_WIDTH])
        kT = jnp.pad(k.transpose(0, 2, 1), ((0, 0), (0, 0), (back_pad, 0)))
        vp = jnp.pad(v, ((0, 0), (back_pad, 0), (0, 0)))
        sl = jnp.asarray(np.repeat(slopes, d) * d, F32)
        o, lse = _band(sl, q, kT, vp, min(256, L), back, 1)
        outs.append(o.reshape(H, d, L, Dh).transpose(0, 2, 1, 3).reshape(H, S, Dh))
        lses.append(lse.reshape(H, d, L, 1).transpose(0, 2, 1, 3).reshape(H, S, 1))
    return _dil_combine(outs, lses, min(512, S))


def _fox_mixer(main, gates, b_f):
    S = main.shape[0]
    H = HEADS_PER_MIXER
    f = gates[:, GATE_COL_F:GATE_COL_F + H].T
    f = jnp.concatenate([f, jnp.zeros_like(f)], axis=0)
    b = jnp.concatenate([b_f, jnp.zeros_like(b_f)]).reshape(2 * H, 1)
    c = _logf_cumsum(f, b, min(512, S))[:H].reshape(H, 1, S)
    return _fox(_heads(main[:, COL_QC:COL_QC + GROUP_WIDTH]), _heads_T(main[:, COL_KC:COL_KC + GROUP_WIDTH]),
                _heads(main[:, COL_VC:COL_VC + GROUP_WIDTH]), c, min(256, S))


def _sb_mixer(main):
    S = main.shape[0]
    return _sb(_heads(main[:, COL_QD:COL_QD + GROUP_WIDTH]), _heads_T(main[:, COL_KD:COL_KD + GROUP_WIDTH]),
               _heads(main[:, COL_VD:COL_VD + GROUP_WIDTH]), min(256, S))


def kernel(x, p, g_mix, w_in, b_nsa_gate, b_forget, cmp_pos_k, cmp_w1_k, cmp_w2_k, cmp_pos_v, cmp_w1_v, cmp_w2_v,
           g_head, w_out, g_mlp, w_up, w_down, g_ple, w_ple_gate, b_ple_gate, w_ple_proj, g_final):
    B, S, D = x.shape
    depth = w_in.shape[0]
    slopes = _alibi_slopes()
    slopes_nsa = tuple(float(s) for s in slopes[0::2])
    slopes_dil = slopes[1::2]
    perm, colscale = _inproj_columns()
    n_pad = N_MAIN + N_GATE - perm.shape[0]

    n_blk = S // NSA_SEL_LEN
    n_cmp_pad = S // NSA_CMP_STRIDE
    cstart = np.arange(n_cmp_pad) * NSA_CMP_STRIDE
    bstart = np.arange(n_blk) * NSA_SEL_LEN
    cover = (cstart[:, None] < bstart[None, :] + NSA_SEL_LEN) & (cstart[:, None] + NSA_CMP_LEN - 1 >= bstart[None, :])
    expand = np.arange(S)[None, :] // NSA_SEL_LEN == np.arange(n_blk)[:, None]
    consts = {"cover": jnp.asarray(cover, BF16), "expand": jnp.asarray(expand, BF16)}
    group = np.arange(D) // HEAD_DIM
    gmean = jnp.asarray((group[:, None] == group[None, :]) / HEAD_DIM, BF16)

    outs = []
    for b in range(B):
        h = x[b]
        for i in range(depth):
            w = jnp.pad(w_in[i][:, perm] * colscale, ((0, 0), (0, n_pad))).astype(BF16)
            main, gates = _inproj(h, g_mix[i].reshape(1, D), w, min(512, S))
            o_a = _nsa_mixer(main, gates, b_nsa_gate[i], cmp_pos_k[i], cmp_w1_k[i], cmp_w2_k[i],
                             cmp_pos_v[i], cmp_w1_v[i], cmp_w2_v[i], slopes_nsa, consts)
            o_b = _dilated_mixer(main, slopes_dil)
            o_c = _fox_mixer(main, gates, b_forget[i])
            o_d = _sb_mixer(main)
            o = jnp.concatenate([o_a, o_b, o_c, o_d], axis=0).transpose(1, 0, 2).reshape(S, D)
            h = _post(h, o, p[i, b], g_head[i].reshape(1, D), gmean, w_out[i].astype(BF16),
                      g_mlp[i].reshape(1, D), w_up[i].astype(BF16), w_down[i].astype(BF16),
                      g_ple[i].reshape(1, D), w_ple_gate[i].astype(BF16), b_ple_gate[i].reshape(1, D),
                      w_ple_proj[i].astype(BF16), min(256, S))
        outs.append(_final_norm(h, g_final.reshape(1, D), min(512, S)))
    return jnp.stack(outs)
```

```python
import functools
import math

import numpy as np
import jax
import jax.numpy as jnp
from jax import lax
from jax.experimental import pallas as pl
from jax.experimental.pallas import tpu as pltpu

F32 = jnp.float32
BF16 = jnp.bfloat16

HEAD_DIM = 64
HEADS_PER_MIXER = 4
GROUP_WIDTH = HEADS_PER_MIXER * HEAD_DIM
N_ALIBI_HEADS = 2 * HEADS_PER_MIXER
NSA_CMP_LEN = 32
NSA_CMP_STRIDE = 16
NSA_SEL_LEN = 64
NSA_N_SEL = 16
NSA_WINDOW = 512
DILATED_PATTERNS = ((128, 1), (512, 4), (2048, 16))
RMS_EPS = 1e-6
SEL_FORCE = 1e9
QK_SCALE = HEAD_DIM ** -0.5

N_MAIN = 2944
N_GATE = 128
COL_QA, COL_KCMP, COL_VCMP, COL_KSEL, COL_VSEL, COL_KWIN, COL_VWIN = 0, 256, 320, 384, 448, 512, 576
COL_QB, COL_KB, COL_VB = 640, 896, 1152
COL_QC, COL_KC, COL_VC = 1408, 1664, 1920
COL_QD, COL_KD, COL_VD = 2176, 2432, 2688
GATE_COL_NSA, GATE_COL_F = 0, 12

NSA_TQ, NSA_TK = 128, 512
FOX_TILE = 512
SB_TILE = 256
EXP_ZERO = -104.0

MASKED = -1e9
M_INIT = -1e8
VMEM_LIMIT = 56 * 1024 * 1024


def _alibi_slopes():
    return 2.0 ** (-8.0 * np.arange(1, N_ALIBI_HEADS + 1) / N_ALIBI_HEADS)


def _params(*sem):
    return pltpu.CompilerParams(dimension_semantics=sem, vmem_limit_bytes=VMEM_LIMIT)


def _resident(shape, index_map):
    return pl.BlockSpec(shape, index_map, pipeline_mode=pl.Buffered(1))


def _dot(a, b):
    return jnp.dot(a, b, preferred_element_type=F32)


def _split3(x):
    hi = x.astype(BF16)
    r = x - hi.astype(F32)
    mid = r.astype(BF16)
    lo = (r - mid.astype(F32)).astype(BF16)
    return hi, mid, lo


def _dot3(x, w01):
    hi, mid, lo = _split3(x)
    return _dot(hi, w01) + _dot(mid, w01) + _dot(lo, w01)


def _rms(x, g):
    return x * lax.rsqrt(jnp.mean(x * x, axis=-1, keepdims=True) + RMS_EPS) * g


def _softplus(z):
    return jnp.maximum(z, 0.0) + jnp.log1p(jnp.exp(-jnp.abs(z)))


def _inproj_kernel(h_ref, g_ref, w_ref, main_ref, gate_ref):
    u = _rms(h_ref[...], g_ref[...]).astype(BF16)
    acc = _dot(u, w_ref[...])
    main_ref[...] = acc[:, :N_MAIN].astype(main_ref.dtype)
    gate_ref[...] = acc[:, N_MAIN:]


def _inproj(h, g, w, tm):
    S, D = h.shape
    n = w.shape[1]
    return pl.pallas_call(
        _inproj_kernel,
        grid=(S // tm,),
        in_specs=[pl.BlockSpec((tm, D), lambda i: (i, 0)),
                  _resident((1, D), lambda i: (0, 0)),
                  _resident((D, n), lambda i: (0, 0))],
        out_specs=[pl.BlockSpec((tm, N_MAIN), lambda i: (i, 0)),
                   pl.BlockSpec((tm, N_GATE), lambda i: (i, 0))],
        out_shape=[jax.ShapeDtypeStruct((S, N_MAIN), BF16), jax.ShapeDtypeStruct((S, N_GATE), F32)],
        compiler_params=_params("parallel"),
        name="inproj",
    )(h, g, w)


def _compress_kernel(flat_ref, pos_ref, w1_ref, w2_ref, out_ref):
    x = (flat_ref[0].astype(F32) + pos_ref[0]).astype(BF16)
    hid = _dot(x, w1_ref[0])
    c = math.sqrt(2.0 / math.pi)
    hid = 0.5 * hid * (1.0 + jnp.tanh(c * (hid + 0.044715 * (hid * hid * hid))))
    out_ref[0] = _dot(hid.astype(BF16), w2_ref[0])


def _compress(flat, pos, w1, w2, tm):
    _, n, width = flat.shape
    hidden = w1.shape[2]
    return pl.pallas_call(
        _compress_kernel,
        grid=(2, n // tm),
        in_specs=[pl.BlockSpec((1, tm, width), lambda a, i: (a, i, 0)),
                  pl.BlockSpec((1, 1, width), lambda a, i: (a, 0, 0)),
                  pl.BlockSpec((1, width, hidden), lambda a, i: (a, 0, 0)),
                  pl.BlockSpec((1, hidden, HEAD_DIM), lambda a, i: (a, 0, 0))],
        out_specs=pl.BlockSpec((1, tm, HEAD_DIM), lambda a, i: (a, i, 0)),
        out_shape=jax.ShapeDtypeStruct((2, n, HEAD_DIM), F32),
        compiler_params=_params("parallel", "parallel"),
        name="nsa_compress",
    )(flat, pos, w1, w2)


def _nsa_cmp_kernel(q_ref, kcT_ref, vc_ref, cover_ref, tile_ref, oc_ref, selb_ref, flag_ref, *, tq, slopes):
    t0 = pl.program_id(0) * tq
    nc = kcT_ref.shape[1]
    nb = cover_ref.shape[1]
    tpos = t0 + lax.broadcasted_iota(jnp.int32, (tq, nc), 0)
    jidx = lax.broadcasted_iota(jnp.int32, (tq, nc), 1)
    mask = jidx * NSA_CMP_STRIDE + (NSA_CMP_LEN - 1) <= tpos
    dist = tpos.astype(F32) - (jidx.astype(F32) * NSA_CMP_STRIDE + (NSA_CMP_LEN - 1) / 2.0)
    kcT = kcT_ref[...]
    vc = vc_ref[...]
    psum = jnp.zeros((tq, nc), F32)
    for h in range(HEADS_PER_MIXER):
        s = _dot(q_ref[h], kcT) - slopes[h] * dist
        m = jnp.max(jnp.where(mask, s, -jnp.inf), axis=-1, keepdims=True)
        m = jnp.where(m > -jnp.inf, m, 0.0)
        e = jnp.where(mask, jnp.exp(s - m), 0.0)
        den = jnp.sum(e, axis=-1, keepdims=True)
        p = e / jnp.maximum(den, 1e-30)
        oc_ref[h] = _dot(p.astype(BF16), vc)
        psum = psum + p
    imp = _dot3(psum, cover_ref[...])
    bidx = lax.broadcasted_iota(jnp.int32, (tq, nb), 1)
    qpos = t0 + lax.broadcasted_iota(jnp.int32, (tq, nb), 0)
    cur = qpos // NSA_SEL_LEN
    forced = jnp.where(bidx == 0, 1.0, jnp.where(bidx == cur, 1.0, jnp.where(bidx == cur - 1, 1.0, 0.0)))
    elig = bidx * NSA_SEL_LEN <= qpos
    score = jnp.where(forced > 0.0, SEL_FORCE, jnp.where(elig, imp, -SEL_FORCE))
    bf = bidx.astype(F32)
    sel = jnp.zeros((tq, nb), F32)
    for _ in range(min(NSA_N_SEL, nb)):
        mx = jnp.max(score, axis=-1, keepdims=True)
        first = jnp.min(jnp.where(score == mx, bf, float(nb)), axis=-1, keepdims=True)
        hit = bf == first
        sel = jnp.where(hit, 1.0, sel)
        score = jnp.where(hit, -jnp.inf, score)
    sel = jnp.where(elig, sel, 0.0)
    selb_ref[...] = jnp.where(sel > 0.0, 0.0, MASKED).astype(selb_ref.dtype)
    any_q = jnp.max(sel, axis=0, keepdims=True).astype(BF16)
    flag_ref[0] = _dot(jnp.broadcast_to(any_q, (8, nb)), tile_ref[...])


def _nsa_cmp(q, kcT, vc, cover, tile_of_block, tq, slopes):
    H, S, Dh = q.shape
    nc = kcT.shape[1]
    nb = cover.shape[1]
    return pl.pallas_call(
        functools.partial(_nsa_cmp_kernel, tq=tq, slopes=slopes),
        grid=(S // tq,),
        in_specs=[pl.BlockSpec((H, tq, Dh), lambda i: (0, i, 0)),
                  _resident((Dh, nc), lambda i: (0, 0)),
                  _resident((nc, Dh), lambda i: (0, 0)),
                  _resident((nc, nb), lambda i: (0, 0)),
                  _resident((nb, 128), lambda i: (0, 0))],
        out_specs=[pl.BlockSpec((H, tq, Dh), lambda i: (0, i, 0)),
                   pl.BlockSpec((tq, nb), lambda i: (i, 0)),
                   pl.BlockSpec((1, 8, 128), lambda i: (i, 0, 0))],
        out_shape=[jax.ShapeDtypeStruct((H, S, Dh), F32), jax.ShapeDtypeStruct((S, nb), BF16),
                   jax.ShapeDtypeStruct((S // tq, 8, 128), F32)],
        compiler_params=_params("parallel"),
        name="nsa_cmp_select",
    )(q, kcT, vc, cover, tile_of_block)


def _nsa_sel_kernel(flag_ref, q_ref, kT_ref, v_ref, selb_ref, expand_ref, o_ref, *, tq, tk, slopes):
    H = HEADS_PER_MIXER
    i = pl.program_id(0)
    t0 = i * tq
    flag_base = i * (kT_ref.shape[1] // tk)
    rows = H * tq
    q = q_ref[...].reshape(rows, HEAD_DIM)
    selb = selb_ref[...]
    selb4 = jnp.concatenate([selb] * H, axis=0)
    slope = jnp.concatenate([jnp.full((tq, 1), slopes[h], F32) for h in range(H)], axis=0)
    r1 = t0 + lax.broadcasted_iota(jnp.int32, (tq, tk), 0)
    rowpos = jnp.concatenate([r1] * H, axis=0)
    col = lax.broadcasted_iota(jnp.int32, (rows, tk), 1)

    def tile(j, carry):
        m, l, acc = carry
        ks = pl.multiple_of(j * tk, tk)
        s = _dot(q, kT_ref[:, pl.ds(ks, tk)]) + _dot(selb4, expand_ref[:, pl.ds(ks, tk)])
        dist = rowpos - (ks + col)
        s = s - slope * dist.astype(F32)
        s = jnp.where(dist >= 0, s, MASKED)
        m_new = jnp.maximum(m, jnp.max(s, axis=-1, keepdims=True))
        p = jnp.exp(s - m_new)
        alpha = jnp.exp(m - m_new)
        l = alpha * l + jnp.sum(p, axis=-1, keepdims=True)
        acc = alpha * acc + _dot(p.astype(BF16), v_ref[pl.ds(ks, tk), :])
        return m_new, l, acc

    def step(j, carry):
        return lax.cond(flag_ref[flag_base + j] > 0, lambda c: tile(j, c), lambda c: c, carry)

    init = (jnp.full((rows, 1), M_INIT, F32), jnp.zeros((rows, 1), F32), jnp.zeros((rows, HEAD_DIM), F32))
    n_tiles = (t0 + tq - 1) // tk + 1
    m, l, acc = lax.fori_loop(0, n_tiles, step, init)
    o_ref[...] = (acc / jnp.maximum(l, 1e-30)).reshape(H, tq, HEAD_DIM)


def _nsa_sel(flags, q, kT, v, selb, expand, tq, tk, slopes):
    H, S, Dh = q.shape
    nb = selb.shape[1]
    grid_spec = pltpu.PrefetchScalarGridSpec(
        num_scalar_prefetch=1,
        grid=(S // tq,),
        in_specs=[pl.BlockSpec((H, tq, Dh), lambda i, fl: (0, i, 0)),
                  _resident((Dh, S), lambda i, fl: (0, 0)),
                  _resident((S, Dh), lambda i, fl: (0, 0)),
                  pl.BlockSpec((tq, nb), lambda i, fl: (i, 0)),
                  _resident((nb, S), lambda i, fl: (0, 0))],
        out_specs=pl.BlockSpec((H, tq, Dh), lambda i, fl: (0, i, 0)),
    )
    return pl.pallas_call(
        functools.partial(_nsa_sel_kernel, tq=tq, tk=tk, slopes=slopes),
        grid_spec=grid_spec,
        out_shape=jax.ShapeDtypeStruct((H, S, Dh), F32),
        compiler_params=_params("parallel"),
        name="nsa_selected",
    )(flags, q, kT, v, selb, expand)


def _band_kernel(slope_ref, q_ref, kT_ref, v_ref, o_ref, lse_ref, *, tq, back, back_pad):
    g = pl.program_id(0)
    q0 = pl.multiple_of(pl.program_id(1) * tq, tq)
    width = back_pad + tq
    s = _dot(q_ref[0], kT_ref[0, :, pl.ds(q0, width)])
    r = lax.broadcasted_iota(jnp.int32, (tq, width), 0)
    c = lax.broadcasted_iota(jnp.int32, (tq, width), 1)
    dist = r + back_pad - c
    valid = jnp.where(dist >= 0, jnp.where(dist <= back, jnp.where(q0 - back_pad + c >= 0, 1.0, 0.0), 0.0), 0.0)
    s = s - slope_ref[g] * dist.astype(F32)
    s = jnp.where(valid > 0.0, s, -jnp.inf)
    m = jnp.max(s, axis=-1, keepdims=True)
    e = jnp.exp(s - m)
    l = jnp.sum(e, axis=-1, keepdims=True)
    o_ref[0] = _dot(e.astype(BF16), v_ref[0, pl.ds(q0, width), :]) / l
    lse_ref[0] = m + jnp.log(l)


def _band(slopes, q, kT, v, tq, back, kv_group):
    G, L, Dh = q.shape
    back_pad = kT.shape[2] - L
    Lp = kT.shape[2]
    grid_spec = pltpu.PrefetchScalarGridSpec(
        num_scalar_prefetch=1,
        grid=(G, L // tq),
        in_specs=[pl.BlockSpec((1, tq, Dh), lambda g, i, sl: (g, i, 0)),
                  pl.BlockSpec((1, Dh, Lp), lambda g, i, sl: (g // kv_group, 0, 0)),
                  pl.BlockSpec((1, Lp, Dh), lambda g, i, sl: (g // kv_group, 0, 0))],
        out_specs=[pl.BlockSpec((1, tq, Dh), lambda g, i, sl: (g, i, 0)),
                   pl.BlockSpec((1, tq, 1), lambda g, i, sl: (g, i, 0))],
    )
    return pl.pallas_call(
        functools.partial(_band_kernel, tq=tq, back=back, back_pad=back_pad),
        grid_spec=grid_spec,
        out_shape=[jax.ShapeDtypeStruct((G, L, Dh), F32), jax.ShapeDtypeStruct((G, L, 1), F32)],
        compiler_params=_params("parallel", "parallel"),
        name="band_attention",
    )(slopes, q, kT, v)


def _nsa_combine_kernel(oc_ref, os_ref, ow_ref, gl_ref, b_ref, out_ref):
    g = jax.nn.sigmoid(gl_ref[...] + b_ref[...])
    out_ref[0] = g[0, 0] * oc_ref[0] + g[1, 0] * os_ref[0] + g[2, 0] * ow_ref[0]


def _nsa_combine(oc, osel, ow, gl, b, tq):
    H, S, Dh = oc.shape
    o_spec = pl.BlockSpec((1, tq, Dh), lambda h, i: (h, i, 0))
    return pl.pallas_call(
        _nsa_combine_kernel,
        grid=(H, S // tq),
        in_specs=[o_spec, o_spec, o_spec,
                  pl.BlockSpec((3, 1, tq, 1), lambda h, i: (0, h, i, 0)),
                  pl.BlockSpec((3, 1, 1, 1), lambda h, i: (0, h, 0, 0))],
        out_specs=o_spec,
        out_shape=jax.ShapeDtypeStruct((H, S, Dh), F32),
        compiler_params=_params("parallel", "parallel"),
        name="nsa_combine",
    )(oc, osel, ow, gl, b)


def _dil_combine_kernel(o1_ref, o2_ref, o3_ref, l1_ref, l2_ref, l3_ref, out_ref):
    l1, l2, l3 = l1_ref[0], l2_ref[0], l3_ref[0]
    m = jnp.maximum(l1, jnp.maximum(l2, l3))
    e1, e2, e3 = jnp.exp(l1 - m), jnp.exp(l2 - m), jnp.exp(l3 - m)
    den = e1 + e2 + e3
    out_ref[0] = (e1 / den) * o1_ref[0] + (e2 / den) * o2_ref[0] + (e3 / den) * o3_ref[0]


def _dil_combine(outs, lses, tq):
    H, S, Dh = outs[0].shape
    o_spec = pl.BlockSpec((1, tq, Dh), lambda h, i: (h, i, 0))
    l_spec = pl.BlockSpec((1, tq, 1), lambda h, i: (h, i, 0))
    return pl.pallas_call(
        _dil_combine_kernel,
        grid=(H, S // tq),
        in_specs=[o_spec] * 3 + [l_spec] * 3,
        out_specs=o_spec,
        out_shape=jax.ShapeDtypeStruct((H, S, Dh), F32),
        compiler_params=_params("parallel", "parallel"),
        name="dilated_combine",
    )(*outs, *lses)


def _logf_cumsum_kernel(f_ref, b_ref, tri_ref, c_ref, carry_ref):
    @pl.when(pl.program_id(0) == 0)
    def _():
        carry_ref[...] = jnp.zeros_like(carry_ref)

    x = f_ref[...] + b_ref[...]
    logf = jnp.minimum(x, 0.0) - jnp.log1p(jnp.exp(-jnp.abs(x)))
    c = _dot3(logf, tri_ref[...]) + carry_ref[:, :1]
    c_ref[...] = c
    carry_ref[...] = jnp.broadcast_to(c[:, -1:], carry_ref.shape)


def _logf_cumsum(f, b, tk):
    R, S = f.shape
    tri = (np.arange(tk)[:, None] <= np.arange(tk)[None, :]).astype(np.float32)
    return pl.pallas_call(
        _logf_cumsum_kernel,
        grid=(S // tk,),
        in_specs=[pl.BlockSpec((R, tk), lambda i: (0, i)),
                  pl.BlockSpec((R, 1), lambda i: (0, 0)),
                  _resident((tk, tk), lambda i: (0, 0))],
        out_specs=pl.BlockSpec((R, tk), lambda i: (0, i)),
        out_shape=jax.ShapeDtypeStruct((R, S), F32),
        scratch_shapes=[pltpu.VMEM((R, 128), F32)],
        compiler_params=_params("arbitrary"),
        name="logf_cumsum",
    )(f, b, jnp.asarray(tri, BF16))


def _fox_kernel(q_ref, kT_ref, v_ref, c_ref, o_ref, kmax_ref, *, tq):
    tk = tq
    i = pl.program_id(1)

    @pl.when(i == 0)
    def _():
        k = kT_ref[0].astype(F32)
        n2 = jnp.max(jnp.sum(k * k, axis=0, keepdims=True), axis=-1, keepdims=True)
        kmax_ref[...] = jnp.broadcast_to(jnp.sqrt(n2), kmax_ref.shape)

    q = q_ref[0]
    qf = q.astype(F32)
    reach = jnp.sqrt(jnp.sum(qf * qf, axis=-1, keepdims=True)) * kmax_ref[:1, :1]
    row = lax.broadcasted_iota(jnp.int32, (tq, tk), 0)
    col = lax.broadcasted_iota(jnp.int32, (tq, tk), 1)

    def tile(j, carry, diagonal):
        m, l, acc = carry
        ks = pl.multiple_of(j * tk, tk)
        c_tile = c_ref[0, :, pl.ds(ks, tk)]
        s = _dot(q, kT_ref[0, :, pl.ds(ks, tk)]) - c_tile
        if diagonal:
            s = jnp.where(col <= row, s, -jnp.inf)
        m_new = jnp.maximum(m, jnp.max(s, axis=-1, keepdims=True))
        p = jnp.exp(s - m_new)
        alpha = jnp.exp(m - m_new)
        l = alpha * l + jnp.sum(p, axis=-1, keepdims=True)
        acc = alpha * acc + _dot(p.astype(BF16), v_ref[0, pl.ds(ks, tk), :])
        gap = jnp.max(reach - c_tile[:, :1] - m_new)
        return (m_new, l, acc), gap

    init = (jnp.full((tq, 1), -jnp.inf, F32), jnp.zeros((tq, 1), F32), jnp.zeros((tq, HEAD_DIM), F32))
    carry, gap = tile(i, init, True)

    def more(state):
        j, gap, _ = state
        return jnp.logical_and(j >= 0, gap >= EXP_ZERO - 1.0)

    def body(state):
        j, _, carry = state
        carry, gap = tile(j, carry, False)
        return j - 1, gap, carry

    _, _, (m, l, acc) = lax.while_loop(more, body, (i - 1, gap, carry))
    o_ref[0] = acc / l


def _fox(q, kT, v, c, tq):
    H, S, Dh = q.shape
    return pl.pallas_call(
        functools.partial(_fox_kernel, tq=tq),
        grid=(H, S // tq),
        in_specs=[pl.BlockSpec((1, tq, Dh), lambda h, i: (h, i, 0)),
                  pl.BlockSpec((1, Dh, S), lambda h, i: (h, 0, 0)),
                  pl.BlockSpec((1, S, Dh), lambda h, i: (h, 0, 0)),
                  pl.BlockSpec((1, 1, S), lambda h, i: (h, 0, 0))],
        out_specs=pl.BlockSpec((1, tq, Dh), lambda h, i: (h, i, 0)),
        out_shape=jax.ShapeDtypeStruct((H, S, Dh), F32),
        scratch_shapes=[pltpu.VMEM((8, 128), F32)],
        compiler_params=_params("parallel", "arbitrary"),
        name="forgetting_attention",
    )(q, kT, v, c)


def _sb_kernel(q_ref, kT_ref, v_ref, tri_ref, o_ref, *, tq):
    tk = tq
    i = pl.program_id(1)
    q = q_ref[0]
    tri = tri_ref[...]
    row = lax.broadcasted_iota(jnp.int32, (tq, tk), 0)
    col = lax.broadcasted_iota(jnp.int32, (tq, tk), 1)

    def tile(j, carry, diagonal):
        run, acc = carry
        ks = pl.multiple_of(j * tk, tk)
        z = _dot(q, kT_ref[0, :, pl.ds(ks, tk)])
        log_keep = -_softplus(z)
        log_beta = z + log_keep
        if diagonal:
            log_keep = jnp.where(col < row, log_keep, 0.0)
        later = run + _dot3(log_keep, tri)
        a = jnp.exp(log_beta + later)
        if diagonal:
            a = jnp.where(col < row, a, 0.0)
        acc = acc + _dot(a.astype(BF16), v_ref[0, pl.ds(ks, tk), :])
        run = run + jnp.sum(log_keep, axis=-1, keepdims=True)
        return run, acc

    init = (jnp.zeros((tq, 1), F32), jnp.zeros((tq, HEAD_DIM), F32))
    run, acc = tile(i, init, True)

    def more(state):
        j, top, _ = state
        return jnp.logical_and(j >= 0, top >= EXP_ZERO)

    def body(state):
        j, _, carry = state
        run, acc = tile(j, carry, False)
        return j - 1, jnp.max(run), (run, acc)

    _, _, (_, acc) = lax.while_loop(more, body, (i - 1, jnp.max(run), (run, acc)))
    o_ref[0] = acc


def _sb(q, kT, v, tq):
    H, S, Dh = q.shape
    tri = (np.arange(tq)[:, None] > np.arange(tq)[None, :]).astype(np.float32)
    return pl.pallas_call(
        functools.partial(_sb_kernel, tq=tq),
        grid=(H, S // tq),
        in_specs=[pl.BlockSpec((1, tq, Dh), lambda h, i: (h, i, 0)),
                  pl.BlockSpec((1, Dh, S), lambda h, i: (h, 0, 0)),
                  pl.BlockSpec((1, S, Dh), lambda h, i: (h, 0, 0)),
                  _resident((tq, tq), lambda h, i: (0, 0))],
        out_specs=pl.BlockSpec((1, tq, Dh), lambda h, i: (h, i, 0)),
        out_shape=jax.ShapeDtypeStruct((H, S, Dh), F32),
        compiler_params=_params("parallel", "parallel"),
        name="stick_breaking_attention",
    )(q, kT, v, jnp.asarray(tri, BF16))


def _post_kernel(h_ref, o_ref, p_ref, ghead_ref, gmean_ref, wout_ref, gmlp_ref, wup_ref, wdown_ref,
                 gple_ref, wgate_ref, bgate_ref, wproj_ref, out_ref, *, f_chunk):
    o = o_ref[...]
    o2 = o * o
    hi = o2.astype(BF16)
    lo = (o2 - hi.astype(F32)).astype(BF16)
    ms = _dot(hi, gmean_ref[...]) + _dot(lo, gmean_ref[...])
    on = o * lax.rsqrt(ms + RMS_EPS) * ghead_ref[...]
    h = h_ref[...] + _dot(on.astype(BF16), wout_ref[...])
    u = _rms(h, gmlp_ref[...]).astype(BF16)
    d_ff = wup_ref.shape[1]
    acc = jnp.zeros(h.shape, F32)
    for f in range(0, d_ff, f_chunk):
        hid = jnp.maximum(_dot(u, wup_ref[:, f:f + f_chunk]), 0.0)
        acc = acc + _dot((hid * hid).astype(BF16), wdown_ref[f:f + f_chunk, :])
    h = h + acc
    gate = jax.nn.sigmoid(_dot(_rms(h, gple_ref[...]).astype(BF16), wgate_ref[...]) + bgate_ref[...])
    out_ref[...] = h + _dot(p_ref[...].astype(BF16), wproj_ref[...]) * gate


def _post(h, o, p, ghead, gmean, wout, gmlp, wup, wdown, gple, wgate, bgate, wproj, tm):
    S, D = h.shape
    row = lambda i: (i, 0)
    fixed = lambda i: (0, 0)
    vec = _resident((1, D), fixed)
    return pl.pallas_call(
        functools.partial(_post_kernel, f_chunk=1024),
        grid=(S // tm,),
        in_specs=[pl.BlockSpec((tm, D), row), pl.BlockSpec((tm, D), row), pl.BlockSpec((tm, p.shape[1]), row),
                  vec, _resident(gmean.shape, fixed), _resident(wout.shape, fixed),
                  vec, _resident(wup.shape, fixed), _resident(wdown.shape, fixed),
                  vec, _resident(wgate.shape, fixed), vec, _resident(wproj.shape, fixed)],
        out_specs=pl.BlockSpec((tm, D), row),
        out_shape=jax.ShapeDtypeStruct((S, D), F32),
        compiler_params=_params("parallel"),
        name="outproj_mlp_ple",
    )(h, o, p, ghead, gmean, wout, gmlp, wup, wdown, gple, wgate, bgate, wproj)


def _final_norm_kernel(h_ref, g_ref, out_ref):
    out_ref[...] = _rms(h_ref[...], g_ref[...])


def _final_norm(h, g, tm):
    S, D = h.shape
    return pl.pallas_call(
        _final_norm_kernel,
        grid=(S // tm,),
        in_specs=[pl.BlockSpec((tm, D), lambda i: (i, 0)), pl.BlockSpec((1, D), lambda i: (0, 0))],
        out_specs=pl.BlockSpec((tm, D), lambda i: (i, 0)),
        out_shape=jax.ShapeDtypeStruct((S, D), F32),
        compiler_params=_params("parallel"),
        name="final_norm",
    )(h, g)


def _inproj_columns():
    splits = (256, 64, 64, 64, 64, 64, 64, 12, 256, 256, 256, 256, 256, 256, 4, 256, 256, 256)
    offs = np.concatenate([[0], np.cumsum(splits)])
    seg = lambda k: np.arange(offs[k], offs[k + 1])
    main = [0, 1, 2, 3, 4, 5, 6, 8, 9, 10, 11, 12, 13, 15, 16, 17]
    perm = np.concatenate([seg(k) for k in main] + [seg(7), seg(14)])
    scale = np.ones(perm.shape[0], np.float32)
    for start in (COL_QA, COL_QB, COL_QC, COL_QD):
        scale[start:start + GROUP_WIDTH] = QK_SCALE
    return perm, scale


def _heads(t):
    S = t.shape[0]
    return t.reshape(S, HEADS_PER_MIXER, HEAD_DIM).transpose(1, 0, 2)


def _heads_T(t):
    S = t.shape[0]
    return t.reshape(S, HEADS_PER_MIXER, HEAD_DIM).transpose(1, 2, 0)


def _nsa_mixer(main, gates, b_gate, pos_k, w1_k, w2_k, pos_v, w1_v, w2_v, slopes, consts):
    S = main.shape[0]
    q = _heads(main[:, COL_QA:COL_QA + GROUP_WIDTH])
    col = lambda c: main[:, c:c + HEAD_DIM]
    n16 = S // NSA_CMP_STRIDE

    def blocks(t):
        t16 = t.reshape(n16, NSA_CMP_STRIDE * HEAD_DIM)
        return jnp.concatenate([t16, jnp.roll(t16, -1, axis=0)], axis=1)

    flat = jnp.stack([blocks(col(COL_KCMP)), blocks(col(COL_VCMP))])
    pos = jnp.stack([pos_k.reshape(1, -1), pos_v.reshape(1, -1)])
    w1 = jnp.stack([w1_k, w1_v]).astype(BF16)
    w2 = jnp.stack([w2_k, w2_v]).astype(BF16)
    kvc = _compress(flat, pos, w1, w2, tm=min(256, n16))
    kcT = kvc[0].T.astype(BF16)
    vc = kvc[1].astype(BF16)
    tq_c = min(NSA_TQ, S)
    tk_s = min(NSA_TK, S)
    o_cmp, selb, flags = _nsa_cmp(q, kcT, vc, consts["cover"], consts["tile_of_block"], tq_c, slopes)
    flags = (flags[:, 0, :S // tk_s] > 0.0).astype(jnp.int32).reshape(-1)
    o_sel = _nsa_sel(flags, q, col(COL_KSEL).T, col(COL_VSEL), selb, consts["expand"], tq_c, tk_s, slopes)
    back = NSA_WINDOW - 1
    back_pad = -(-back // 128) * 128
    kT_w = jnp.pad(col(COL_KWIN).T, ((0, 0), (back_pad, 0)))[None]
    v_w = jnp.pad(col(COL_VWIN), ((back_pad, 0), (0, 0)))[None]
    o_win, _ = _band(jnp.asarray(slopes, F32), q, kT_w, v_w, min(256, S), back, HEADS_PER_MIXER)
    gl = gates[:, GATE_COL_NSA:GATE_COL_NSA + 3 * HEADS_PER_MIXER]
    gl = gl.reshape(S, 3, HEADS_PER_MIXER).transpose(1, 2, 0)[..., None]
    return _nsa_combine(o_cmp, o_sel, o_win, gl, b_gate.reshape(3, HEADS_PER_MIXER, 1, 1), min(512, S))


def _dilated_mixer(main, slopes):
    S = main.shape[0]
    H, Dh = HEADS_PER_MIXER, HEAD_DIM
    outs, lses = [], []
    for window, d in DILATED_PATTERNS:
        L = S // d
        back = window // d
        back_pad = -(-back // 128) * 128

        def sub(t):
            return t.reshape(L, d, H, Dh).transpose(2, 1, 0, 3).reshape(H * d, L, Dh)

        q = sub(main[:, COL_QB:COL_QB + GROUP_WIDTH])
        k = sub(main[:, COL_KB:COL_KB + GROUP_WIDTH])
        v = sub(main[:, COL_VB:COL_VB + GROUP_WIDTH])
        kT = jnp.pad(k.transpose(0, 2, 1), ((0, 0), (0, 0), (back_pad, 0)))
        vp = jnp.pad(v, ((0, 0), (back_pad, 0), (0, 0)))
        sl = jnp.asarray(np.repeat(slopes, d) * d, F32)
        o, lse = _band(sl, q, kT, vp, min(256, L), back, 1)
        outs.append(o.reshape(H, d, L, Dh).transpose(0, 2, 1, 3).reshape(H, S, Dh))
        lses.append(lse.reshape(H, d, L, 1).transpose(0, 2, 1, 3).reshape(H, S, 1))
    return _dil_combine(outs, lses, min(512, S))


def _fox_mixer(main, gates, b_f):
    S = main.shape[0]
    H = HEADS_PER_MIXER
    f = gates[:, GATE_COL_F:GATE_COL_F + H].T
    f = jnp.concatenate([f, jnp.zeros_like(f)], axis=0)
    b = jnp.concatenate([b_f, jnp.zeros_like(b_f)]).reshape(2 * H, 1)
    c = _logf_cumsum(f, b, min(512, S))[:H].reshape(H, 1, S)
    return _fox(_heads(main[:, COL_QC:COL_QC + GROUP_WIDTH]), _heads_T(main[:, COL_KC:COL_KC + GROUP_WIDTH]),
                _heads(main[:, COL_VC:COL_VC + GROUP_WIDTH]), c, min(FOX_TILE, S))


def _sb_mixer(main):
    S = main.shape[0]
    return _sb(_heads(main[:, COL_QD:COL_QD + GROUP_WIDTH]), _heads_T(main[:, COL_KD:COL_KD + GROUP_WIDTH]),
               _heads(main[:, COL_VD:COL_VD + GROUP_WIDTH]), min(SB_TILE, S))


def kernel(x, p, g_mix, w_in, b_nsa_gate, b_forget, cmp_pos_k, cmp_w1_k, cmp_w2_k, cmp_pos_v, cmp_w1_v, cmp_w2_v,
           g_head, w_out, g_mlp, w_up, w_down, g_ple, w_ple_gate, b_ple_gate, w_ple_proj, g_final):
    B, S, D = x.shape
    depth = w_in.shape[0]
    slopes = _alibi_slopes()
    slopes_nsa = tuple(float(s) for s in slopes[0::2])
    slopes_dil = slopes[1::2]
    perm, colscale = _inproj_columns()
    n_pad = N_MAIN + N_GATE - perm.shape[0]

    n_blk = S // NSA_SEL_LEN
    n_cmp_pad = S // NSA_CMP_STRIDE
    cstart = np.arange(n_cmp_pad) * NSA_CMP_STRIDE
    bstart = np.arange(n_blk) * NSA_SEL_LEN
    cover = (cstart[:, None] < bstart[None, :] + NSA_SEL_LEN) & (cstart[:, None] + NSA_CMP_LEN - 1 >= bstart[None, :])
    expand = np.arange(S)[None, :] // NSA_SEL_LEN == np.arange(n_blk)[:, None]
    tile_of_block = bstart[:, None] // min(NSA_TK, S) == np.arange(128)[None, :]
    consts = {"cover": jnp.asarray(cover, BF16), "expand": jnp.asarray(expand, BF16),
              "tile_of_block": jnp.asarray(tile_of_block, BF16)}
    group = np.arange(D) // HEAD_DIM
    gmean = jnp.asarray((group[:, None] == group[None, :]) / HEAD_DIM, BF16)

    outs = []
    for b in range(B):
        h = x[b]
        for i in range(depth):
            w = jnp.pad(w_in[i][:, perm] * colscale, ((0, 0), (0, n_pad))).astype(BF16)
            main, gates = _inproj(h, g_mix[i].reshape(1, D), w, min(512, S))
            o_a = _nsa_mixer(main, gates, b_nsa_gate[i], cmp_pos_k[i], cmp_w1_k[i], cmp_w2_k[i],
                             cmp_pos_v[i], cmp_w1_v[i], cmp_w2_v[i], slopes_nsa, consts)
            o_b = _dilated_mixer(main, slopes_dil)
            o_c = _fox_mixer(main, gates, b_forget[i])
            o_d = _sb_mixer(main)
            o = jnp.concatenate([o_a, o_b, o_c, o_d], axis=0).transpose(1, 0, 2).reshape(S, D)
            h = _post(h, o, p[i, b], g_head[i].reshape(1, D), gmean, w_out[i].astype(BF16),
                      g_mlp[i].reshape(1, D), w_up[i].astype(BF16), w_down[i].astype(BF16),
                      g_ple[i].reshape(1, D), w_ple_gate[i].astype(BF16), b_ple_gate[i].reshape(1, D),
                      w_ple_proj[i].astype(BF16), min(256, S))
        outs.append(_final_norm(h, g_final.reshape(1, D), min(512, S)))
    return jnp.stack(outs)
```

```python
import functools
import math

import numpy as np
import jax
import jax.numpy as jnp
from jax import lax
from jax.experimental import pallas as pl
from jax.experimental.pallas import tpu as pltpu

F32 = jnp.float32
BF16 = jnp.bfloat16

HEAD_DIM = 64
HEADS_PER_MIXER = 4
GROUP_WIDTH = HEADS_PER_MIXER * HEAD_DIM
PAIR = 2 * HEAD_DIM
N_ALIBI_HEADS = 2 * HEADS_PER_MIXER
NSA_CMP_LEN = 32
NSA_CMP_STRIDE = 16
NSA_CMP_HIDDEN = 256
NSA_SEL_LEN = 64
NSA_N_SEL = 16
NSA_WINDOW = 512
DILATED_PATTERNS = ((128, 1), (512, 4), (2048, 16))
RMS_EPS = 1e-6
SEL_FORCE = 1e9
QK_SCALE = HEAD_DIM ** -0.5

N_MAIN = 2944
N_GATE = 128
COL_QA = 0
COL_KVCMP, COL_KVSEL, COL_KVWIN = 256, 384, 512
COL_QB, COL_KB, COL_VB = 640, 896, 1152
COL_QC, COL_KC, COL_VC = 1408, 1664, 1920
COL_QD, COL_KD, COL_VD = 2176, 2432, 2688
GATE_COL_NSA, GATE_COL_F = 0, 12

NSA_CQ = 256
NSA_TQ, NSA_TK = 128, 512
FOX_TILE = 512
SB_TILE = 256
BAND_TQ = 512
EXP_ZERO = -104.0

MASKED = -1e9
M_INIT = -1e8
VMEM_LIMIT = 56 * 1024 * 1024


def _alibi_slopes():
    return 2.0 ** (-8.0 * np.arange(1, N_ALIBI_HEADS + 1) / N_ALIBI_HEADS)


def _params(*sem):
    return pltpu.CompilerParams(dimension_semantics=sem, vmem_limit_bytes=VMEM_LIMIT)


def _resident(shape, index_map):
    return pl.BlockSpec(shape, index_map, pipeline_mode=pl.Buffered(1))


def _dot(a, b):
    return jnp.dot(a, b, preferred_element_type=F32)


def _dot_nt(a, b):
    return lax.dot_general(a, b, (((1,), (1,)), ((), ())), preferred_element_type=F32)


def _split3(x):
    hi = x.astype(BF16)
    r = x - hi.astype(F32)
    mid = r.astype(BF16)
    lo = (r - mid.astype(F32)).astype(BF16)
    return hi, mid, lo


def _dot3(x, w01):
    hi, mid, lo = _split3(x)
    return _dot(hi, w01) + _dot(mid, w01) + _dot(lo, w01)


def _rms(x, g):
    return x * lax.rsqrt(jnp.mean(x * x, axis=-1, keepdims=True) + RMS_EPS) * g


def _softplus(z):
    return jnp.maximum(z, 0.0) + jnp.log1p(jnp.exp(-jnp.abs(z)))


def _head_lanes(shape, hh):
    lane = lax.broadcasted_iota(jnp.int32, shape, len(shape) - 1)
    return lane < HEAD_DIM if hh == 0 else lane >= HEAD_DIM


def _inproj_kernel(h_ref, g_ref, w_ref, main_ref, gate_ref):
    u = _rms(h_ref[...], g_ref[...]).astype(BF16)
    acc = _dot(u, w_ref[...])
    main_ref[...] = acc[:, :N_MAIN].astype(main_ref.dtype)
    gate_ref[...] = acc[:, N_MAIN:]


def _inproj(h, g, w, tm):
    S, D = h.shape
    n = w.shape[1]
    return pl.pallas_call(
        _inproj_kernel,
        grid=(S // tm,),
        in_specs=[pl.BlockSpec((tm, D), lambda i: (i, 0)),
                  _resident((1, D), lambda i: (0, 0)),
                  _resident((D, n), lambda i: (0, 0))],
        out_specs=[pl.BlockSpec((tm, N_MAIN), lambda i: (i, 0)),
                   pl.BlockSpec((tm, N_GATE), lambda i: (i, 0))],
        out_shape=[jax.ShapeDtypeStruct((S, N_MAIN), BF16), jax.ShapeDtypeStruct((S, N_GATE), F32)],
        compiler_params=_params("parallel"),
        name="inproj",
    )(h, g, w)


def _compress_kernel(x_ref, pos_ref, w1_ref, w2_ref, out_ref):
    x = x_ref[...].astype(F32)
    n = x.shape[0]
    top = _dot((x + pos_ref[0:1, :]).astype(BF16), w1_ref[0])
    bot = _dot((x + pos_ref[1:2, :]).astype(BF16), w1_ref[1])
    hid = top + pltpu.roll(bot, n - 1, 0)
    c = math.sqrt(2.0 / math.pi)
    hid = 0.5 * hid * (1.0 + jnp.tanh(c * (hid + 0.044715 * (hid * hid * hid))))
    out_ref[...] = _dot(hid.astype(BF16), w2_ref[...]).astype(out_ref.dtype)


def _compress(x16, pos, w1, w2):
    n = x16.shape[0]
    return pl.pallas_call(
        _compress_kernel,
        out_shape=jax.ShapeDtypeStruct((n, PAIR), BF16),
        compiler_params=pltpu.CompilerParams(vmem_limit_bytes=VMEM_LIMIT),
        name="nsa_compress",
    )(x16, pos, w1, w2)


def _nsa_cmp_kernel(q_ref, kvc_ref, cover_ref, tile_ref, oc_ref, selb_ref, flag_ref, *, tq, slopes):
    t0 = pl.program_id(0) * tq
    nc = kvc_ref.shape[0]
    nb = cover_ref.shape[1]
    tpos = t0 + lax.broadcasted_iota(jnp.int32, (tq, nc), 0)
    jidx = lax.broadcasted_iota(jnp.int32, (tq, nc), 1)
    mask = jidx * NSA_CMP_STRIDE + (NSA_CMP_LEN - 1) <= tpos
    dist = tpos.astype(F32) - (jidx.astype(F32) * NSA_CMP_STRIDE + (NSA_CMP_LEN - 1) / 2.0)
    kvc = kvc_ref[...]
    q = q_ref[...]
    zeros = jnp.zeros((tq, HEAD_DIM), BF16)
    psum = jnp.zeros((tq, nc), F32)
    outs = []
    for h in range(HEADS_PER_MIXER):
        qh = jnp.concatenate([q[:, h * HEAD_DIM:(h + 1) * HEAD_DIM], zeros], axis=1)
        s = _dot_nt(qh, kvc) - slopes[h] * dist
        m = jnp.max(jnp.where(mask, s, -jnp.inf), axis=-1, keepdims=True)
        m = jnp.where(m > -jnp.inf, m, 0.0)
        e = jnp.where(mask, jnp.exp(s - m), 0.0)
        den = jnp.sum(e, axis=-1, keepdims=True)
        p = e / jnp.maximum(den, 1e-30)
        outs.append(_dot(p.astype(BF16), kvc)[:, HEAD_DIM:])
        psum = psum + p
    oc_ref[...] = jnp.concatenate(outs, axis=1)
    imp = _dot3(psum, cover_ref[...])
    bidx = lax.broadcasted_iota(jnp.int32, (tq, nb), 1)
    qpos = t0 + lax.broadcasted_iota(jnp.int32, (tq, nb), 0)
    cur = qpos // NSA_SEL_LEN
    forced = jnp.where(bidx == 0, 1.0, jnp.where(bidx == cur, 1.0, jnp.where(bidx == cur - 1, 1.0, 0.0)))
    elig = bidx * NSA_SEL_LEN <= qpos
    score = jnp.where(forced > 0.0, SEL_FORCE, jnp.where(elig, imp, -SEL_FORCE))
    bf = bidx.astype(F32)
    sel = jnp.zeros((tq, nb), F32)
    for _ in range(min(NSA_N_SEL, nb)):
        mx = jnp.max(score, axis=-1, keepdims=True)
        first = jnp.min(jnp.where(score == mx, bf, float(nb)), axis=-1, keepdims=True)
        hit = bf == first
        sel = jnp.where(hit, 1.0, sel)
        score = jnp.where(hit, -jnp.inf, score)
    sel = jnp.where(elig, sel, 0.0)
    selb_ref[...] = jnp.where(sel > 0.0, 0.0, MASKED).astype(selb_ref.dtype)
    for part in range(tq // NSA_TQ):
        any_q = jnp.max(sel[part * NSA_TQ:(part + 1) * NSA_TQ], axis=0, keepdims=True).astype(BF16)
        flag_ref[part] = _dot(jnp.broadcast_to(any_q, (8, nb)), tile_ref[...])


def _nsa_cmp(main, kvc, cover, tile_of_block, tq, slopes):
    S = main.shape[0]
    nc = kvc.shape[0]
    nb = cover.shape[1]
    parts = tq // NSA_TQ
    return pl.pallas_call(
        functools.partial(_nsa_cmp_kernel, tq=tq, slopes=slopes),
        grid=(S // tq,),
        in_specs=[pl.BlockSpec((tq, GROUP_WIDTH), lambda i: (i, COL_QA // GROUP_WIDTH)),
                  _resident((nc, PAIR), lambda i: (0, 0)),
                  _resident((nc, nb), lambda i: (0, 0)),
                  _resident((nb, 128), lambda i: (0, 0))],
        out_specs=[pl.BlockSpec((tq, GROUP_WIDTH), lambda i: (i, 0)),
                   pl.BlockSpec((tq, nb), lambda i: (i, 0)),
                   pl.BlockSpec((parts, 8, 128), lambda i: (i, 0, 0))],
        out_shape=[jax.ShapeDtypeStruct((S, GROUP_WIDTH), F32), jax.ShapeDtypeStruct((S, nb), BF16),
                   jax.ShapeDtypeStruct((S // NSA_TQ, 8, 128), F32)],
        compiler_params=_params("parallel"),
        name="nsa_cmp_select",
    )(main, kvc, cover, tile_of_block)


def _nsa_attend_kernel(flag_ref, q_ref, rhs_ref, kvsel_ref, kvwin_ref, selb_ref, ocmp_ref, gate_ref, bgate_ref,
                       o_ref, lhs_ref, m_ref, l_ref, acc_ref, *, tq, tk, slopes):
    H = HEADS_PER_MIXER
    S = kvsel_ref.shape[0]
    i = pl.program_id(0)
    t0 = i * tq
    flag_base = i * (S // tk)
    rows = H * tq
    nb = selb_ref.shape[1]

    q = q_ref[...]
    selb = selb_ref[...]
    lane = lax.broadcasted_iota(jnp.int32, (tq, HEAD_DIM), 1)
    for h in range(H):
        aug = jnp.where(lane == 0, 128.0 * slopes[h], jnp.where(lane == 1, slopes[h], 0.0)).astype(BF16)
        lhs_ref[h * tq:(h + 1) * tq, 0:PAIR] = jnp.concatenate([q[:, h * HEAD_DIM:(h + 1) * HEAD_DIM], aug], axis=1)
        lhs_ref[h * tq:(h + 1) * tq, PAIR:PAIR + nb] = selb
    m_ref[...] = jnp.full(m_ref.shape, M_INIT, F32)
    l_ref[...] = jnp.zeros(l_ref.shape, F32)
    acc_ref[...] = jnp.zeros(acc_ref.shape, F32)

    def tile(j, diagonal):
        ks = pl.multiple_of(j * tk, tk)
        rhs = rhs_ref[:, pl.ds(ks, tk)]
        kv = kvsel_ref[pl.ds(ks, tk), :]
        half = rows // 2
        for rs in (slice(0, half), slice(half, rows)):
            s = _dot(lhs_ref[rs, :], rhs)
            if diagonal:
                qpos = jnp.concatenate([t0 + lax.broadcasted_iota(jnp.int32, (tq, tk), 0)] * (H // 2), axis=0)
                kpos = ks + lax.broadcasted_iota(jnp.int32, (half, tk), 1)
                s = jnp.where(kpos <= qpos, s, MASKED)
            m_old = m_ref[rs, :]
            m_new = jnp.maximum(m_old, jnp.max(s, axis=-1, keepdims=True))
            p = jnp.exp(s - m_new)
            alpha = jnp.exp(m_old - m_new)
            l_ref[rs, :] = alpha * l_ref[rs, :] + jnp.sum(p, axis=-1, keepdims=True)
            acc_ref[rs, :] = alpha * acc_ref[rs, :] + _dot(p.astype(BF16), kv)
            m_ref[rs, :] = m_new

    last = (t0 + tq - 1) // tk

    def step(j, _):
        @pl.when(flag_ref[flag_base + j] > 0)
        def _():
            tile(j, False)
        return 0

    lax.fori_loop(0, last, step, 0)
    tile(last, True)
    o_sel = acc_ref[:, HEAD_DIM:] / jnp.maximum(l_ref[...], 1e-30)

    back = NSA_WINDOW - 1
    width = NSA_WINDOW + tq
    start = pl.multiple_of(jnp.maximum(t0 - NSA_WINDOW, 0), tq)
    kw = kvwin_ref[pl.ds(start, width), :]
    qz = jnp.where(_head_lanes((rows, PAIR), 0), lhs_ref[:, 0:PAIR], jnp.zeros((rows, PAIR), BF16))
    sw = _dot_nt(qz, kw)
    qpos = jnp.concatenate([t0 + lax.broadcasted_iota(jnp.int32, (tq, width), 0)] * H, axis=0)
    dist = qpos - (start + lax.broadcasted_iota(jnp.int32, (rows, width), 1))
    slope = jnp.concatenate([jnp.full((tq, 1), slopes[h], F32) for h in range(H)], axis=0)
    sw = sw - slope * dist.astype(F32)
    sw = jnp.where(dist >= 0, jnp.where(dist <= back, sw, -jnp.inf), -jnp.inf)
    mw = jnp.max(sw, axis=-1, keepdims=True)
    ew = jnp.exp(sw - mw)
    o_win = _dot(ew.astype(BF16), kw)[:, HEAD_DIM:] / jnp.sum(ew, axis=-1, keepdims=True)

    g = jax.nn.sigmoid(gate_ref[...] + bgate_ref[...])
    ocmp = ocmp_ref[...]
    outs = []
    for h in range(H):
        r0 = h * tq
        c0 = GATE_COL_NSA + h
        outs.append(g[:, c0:c0 + 1] * ocmp[:, h * HEAD_DIM:(h + 1) * HEAD_DIM]
                    + g[:, c0 + H:c0 + H + 1] * o_sel[r0:r0 + tq]
                    + g[:, c0 + 2 * H:c0 + 2 * H + 1] * o_win[r0:r0 + tq])
    o_ref[...] = jnp.concatenate(outs, axis=1)


def _nsa_attend(flags, main, rhs, selb, ocmp, gates, bgate, tq, tk, slopes):
    S = main.shape[0]
    nb = selb.shape[1]
    rows = HEADS_PER_MIXER * tq
    grid_spec = pltpu.PrefetchScalarGridSpec(
        num_scalar_prefetch=1,
        grid=(S // tq,),
        in_specs=[pl.BlockSpec((tq, GROUP_WIDTH), lambda i, fl: (i, COL_QA // GROUP_WIDTH)),
                  _resident((rhs.shape[0], S), lambda i, fl: (0, 0)),
                  _resident((S, PAIR), lambda i, fl: (0, COL_KVSEL // PAIR)),
                  _resident((S, PAIR), lambda i, fl: (0, COL_KVWIN // PAIR)),
                  pl.BlockSpec((tq, nb), lambda i, fl: (i, 0)),
                  pl.BlockSpec((tq, GROUP_WIDTH), lambda i, fl: (i, 0)),
                  pl.BlockSpec((tq, N_GATE), lambda i, fl: (i, 0)),
                  _resident((1, N_GATE), lambda i, fl: (0, 0))],
        out_specs=pl.BlockSpec((tq, GROUP_WIDTH), lambda i, fl: (i, 0)),
        scratch_shapes=[pltpu.VMEM((rows, PAIR + nb), BF16), pltpu.VMEM((rows, 1), F32),
                        pltpu.VMEM((rows, 1), F32), pltpu.VMEM((rows, PAIR), F32)],
    )
    return pl.pallas_call(
        functools.partial(_nsa_attend_kernel, tq=tq, tk=tk, slopes=slopes),
        grid_spec=grid_spec,
        out_shape=jax.ShapeDtypeStruct((S, GROUP_WIDTH), F32),
        compiler_params=_params("parallel"),
        name="nsa_attend",
    )(flags, main, rhs, main, main, selb, ocmp, gates, bgate)


def _band_kernel(q_ref, k_ref, v_ref, o_ref, lse_ref, *, tq, back, slopes, d):
    pr = pl.program_id(1)
    q0 = pl.multiple_of(pl.program_id(2) * tq, tq)
    back_pad = -(-back // 128) * 128
    width = back_pad + tq
    start = pl.multiple_of(jnp.maximum(q0 - back_pad, 0), 128)
    k = k_ref[pl.ds(start, width), :]
    v = v_ref[pl.ds(start, width), :]
    q = q_ref[...]
    dist = (q0 + lax.broadcasted_iota(jnp.int32, (tq, width), 0)) - (start + lax.broadcasted_iota(jnp.int32, (tq, width), 1))
    valid = jnp.where(dist >= 0, jnp.where(dist <= back, 1.0, 0.0), 0.0)
    distf = dist.astype(F32)
    o_pair, lse_pair = [], []
    for hh in range(2):
        slope = jnp.where(pr == 0, slopes[hh], slopes[2 + hh]) * d
        qh = jnp.where(_head_lanes(q.shape, hh), q, jnp.zeros_like(q))
        s = _dot_nt(qh, k) - slope * distf
        s = jnp.where(valid > 0.0, s, -jnp.inf)
        m = jnp.max(s, axis=-1, keepdims=True)
        e = jnp.exp(s - m)
        l = jnp.sum(e, axis=-1, keepdims=True)
        o_pair.append(_dot(e.astype(BF16), v) / l)
        lse_pair.append(m + jnp.log(l))
    first = _head_lanes((tq, PAIR), 0)
    o_ref[...] = jnp.where(first, o_pair[0], o_pair[1])
    lse_ref[...] = jnp.where(first, lse_pair[0], lse_pair[1])


def _band(main, d, back, slopes, tq):
    S = main.shape[0]
    L = S // d
    view = main.reshape(L, d * N_MAIN)
    per_r = N_MAIN // PAIR
    spec = lambda col: pl.BlockSpec((tq, PAIR), lambda r, p_, i: (i, r * per_r + col // PAIR + p_))
    whole = lambda col: pl.BlockSpec((L, PAIR), lambda r, p_, i: (0, r * per_r + col // PAIR + p_))
    out_spec = pl.BlockSpec((tq, PAIR), lambda r, p_, i: (i, r * 2 + p_))
    o, lse = pl.pallas_call(
        functools.partial(_band_kernel, tq=tq, back=back, slopes=slopes, d=float(d)),
        grid=(d, 2, L // tq),
        in_specs=[spec(COL_QB), whole(COL_KB), whole(COL_VB)],
        out_specs=[out_spec, out_spec],
        out_shape=[jax.ShapeDtypeStruct((L, d * GROUP_WIDTH), F32)] * 2,
        compiler_params=_params("parallel", "parallel", "parallel"),
        name="band_attention",
    )(view, view, view)
    return o.reshape(S, GROUP_WIDTH), lse.reshape(S, GROUP_WIDTH)


def _dil_combine_kernel(o1_ref, o2_ref, o3_ref, l1_ref, l2_ref, l3_ref, out_ref):
    l1, l2, l3 = l1_ref[...], l2_ref[...], l3_ref[...]
    m = jnp.maximum(l1, jnp.maximum(l2, l3))
    e1, e2, e3 = jnp.exp(l1 - m), jnp.exp(l2 - m), jnp.exp(l3 - m)
    den = e1 + e2 + e3
    out_ref[...] = (e1 / den) * o1_ref[...] + (e2 / den) * o2_ref[...] + (e3 / den) * o3_ref[...]


def _dil_combine(outs, lses, tm):
    S, W = outs[0].shape
    spec = pl.BlockSpec((tm, W), lambda i: (i, 0))
    return pl.pallas_call(
        _dil_combine_kernel,
        grid=(S // tm,),
        in_specs=[spec] * 6,
        out_specs=spec,
        out_shape=jax.ShapeDtypeStruct((S, W), F32),
        compiler_params=_params("parallel"),
        name="dilated_combine",
    )(*outs, *lses)


def _logf_cumsum_kernel(f_ref, b_ref, tri_ref, c_ref, carry_ref):
    @pl.when(pl.program_id(0) == 0)
    def _():
        carry_ref[...] = jnp.zeros_like(carry_ref)

    x = f_ref[...] + b_ref[...]
    logf = jnp.minimum(x, 0.0) - jnp.log1p(jnp.exp(-jnp.abs(x)))
    c = _dot3(logf, tri_ref[...]) + carry_ref[:, :1]
    c_ref[...] = c
    carry_ref[...] = jnp.broadcast_to(c[:, -1:], carry_ref.shape)


def _logf_cumsum(f, b, tk):
    R, S = f.shape
    tri = (np.arange(tk)[:, None] <= np.arange(tk)[None, :]).astype(np.float32)
    return pl.pallas_call(
        _logf_cumsum_kernel,
        grid=(S // tk,),
        in_specs=[pl.BlockSpec((R, tk), lambda i: (0, i)),
                  pl.BlockSpec((R, 1), lambda i: (0, 0)),
                  _resident((tk, tk), lambda i: (0, 0))],
        out_specs=pl.BlockSpec((R, tk), lambda i: (0, i)),
        out_shape=jax.ShapeDtypeStruct((R, S), F32),
        scratch_shapes=[pltpu.VMEM((R, 128), F32)],
        compiler_params=_params("arbitrary"),
        name="logf_cumsum",
    )(f, b, jnp.asarray(tri, BF16))


def _pair_key_norms(k_ref, kmax_ref, chunk):
    S = k_ref.shape[0]
    first = _head_lanes((chunk, PAIR), 0)

    def body(ci, best):
        k = k_ref[pl.ds(pl.multiple_of(ci * chunk, chunk), chunk), :].astype(F32)
        sq = k * k
        n0 = jnp.max(jnp.sum(jnp.where(first, sq, 0.0), axis=-1, keepdims=True), axis=0, keepdims=True)
        n1 = jnp.max(jnp.sum(jnp.where(first, 0.0, sq), axis=-1, keepdims=True), axis=0, keepdims=True)
        return jnp.maximum(best[0], n0), jnp.maximum(best[1], n1)

    zero = jnp.zeros((1, 1), F32)
    n0, n1 = lax.fori_loop(0, S // chunk, body, (zero, zero))
    kmax_ref[0:1, :] = jnp.broadcast_to(jnp.sqrt(n0), (1, kmax_ref.shape[1]))
    kmax_ref[1:2, :] = jnp.broadcast_to(jnp.sqrt(n1), (1, kmax_ref.shape[1]))


def _fox_kernel(q_ref, k_ref, v_ref, c_ref, o_ref, kmax_ref, *, tq):
    tk = tq
    i = pl.program_id(1)

    @pl.when(i == 0)
    def _():
        _pair_key_norms(k_ref, kmax_ref, min(1024, k_ref.shape[0]))

    q = q_ref[...]
    row = lax.broadcasted_iota(jnp.int32, (tq, tk), 0)
    col = lax.broadcasted_iota(jnp.int32, (tq, tk), 1)
    qs, reach = [], []
    for hh in range(2):
        qh = jnp.where(_head_lanes(q.shape, hh), q, jnp.zeros_like(q))
        qf = qh.astype(F32)
        qs.append(qh)
        reach.append(jnp.sqrt(jnp.sum(qf * qf, axis=-1, keepdims=True)) * kmax_ref[hh:hh + 1, :1])

    def tile(j, carries, diagonal):
        ks = pl.multiple_of(j * tk, tk)
        k = k_ref[pl.ds(ks, tk), :]
        v = v_ref[pl.ds(ks, tk), :]
        new, gap = [], None
        for hh in range(2):
            m, l, acc = carries[hh]
            c_tile = c_ref[0, hh:hh + 1, pl.ds(ks, tk)]
            s = _dot_nt(qs[hh], k) - c_tile
            if diagonal:
                s = jnp.where(col <= row, s, -jnp.inf)
            m_new = jnp.maximum(m, jnp.max(s, axis=-1, keepdims=True))
            p = jnp.exp(s - m_new)
            alpha = jnp.exp(m - m_new)
            l = alpha * l + jnp.sum(p, axis=-1, keepdims=True)
            acc = alpha * acc + _dot(p.astype(BF16), v)
            new.append((m_new, l, acc))
            g = jnp.max(reach[hh] - c_tile[:, :1] - m_new)
            gap = g if gap is None else jnp.maximum(gap, g)
        return tuple(new), gap

    init = (jnp.full((tq, 1), -jnp.inf, F32), jnp.zeros((tq, 1), F32), jnp.zeros((tq, PAIR), F32))
    carries, gap = tile(i, (init, init), True)

    def more(state):
        j, gap, _ = state
        return jnp.logical_and(j >= 0, gap >= EXP_ZERO - 1.0)

    def body(state):
        j, _, carries = state
        carries, gap = tile(j, carries, False)
        return j - 1, gap, carries

    _, _, carries = lax.while_loop(more, body, (i - 1, gap, carries))
    outs = [acc / l for _, l, acc in carries]
    o_ref[...] = jnp.where(_head_lanes((tq, PAIR), 0), outs[0], outs[1])


def _pair_specs(S, tq, col_q, col_k, col_v):
    return [pl.BlockSpec((tq, PAIR), lambda p_, i: (i, col_q // PAIR + p_)),
            pl.BlockSpec((S, PAIR), lambda p_, i: (0, col_k // PAIR + p_)),
            pl.BlockSpec((S, PAIR), lambda p_, i: (0, col_v // PAIR + p_))]


def _fox(main, c, tq):
    S = main.shape[0]
    return pl.pallas_call(
        functools.partial(_fox_kernel, tq=tq),
        grid=(2, S // tq),
        in_specs=_pair_specs(S, tq, COL_QC, COL_KC, COL_VC) + [pl.BlockSpec((1, 2, S), lambda p_, i: (p_, 0, 0))],
        out_specs=pl.BlockSpec((tq, PAIR), lambda p_, i: (i, p_)),
        out_shape=jax.ShapeDtypeStruct((S, GROUP_WIDTH), F32),
        scratch_shapes=[pltpu.VMEM((8, 128), F32)],
        compiler_params=_params("parallel", "arbitrary"),
        name="forgetting_attention",
    )(main, main, main, c)


def _sb_kernel(q_ref, k_ref, v_ref, tri_ref, o_ref, *, tq):
    tk = tq
    i = pl.program_id(1)
    q = q_ref[...]
    tri = tri_ref[...]
    row = lax.broadcasted_iota(jnp.int32, (tq, tk), 0)
    col = lax.broadcasted_iota(jnp.int32, (tq, tk), 1)
    qs = [jnp.where(_head_lanes(q.shape, hh), q, jnp.zeros_like(q)) for hh in range(2)]

    def tile(j, carries, diagonal):
        ks = pl.multiple_of(j * tk, tk)
        k = k_ref[pl.ds(ks, tk), :]
        v = v_ref[pl.ds(ks, tk), :]
        new, top = [], None
        for hh in range(2):
            run, acc = carries[hh]
            z = _dot_nt(qs[hh], k)
            log_keep = -_softplus(z)
            log_beta = z + log_keep
            if diagonal:
                log_keep = jnp.where(col < row, log_keep, 0.0)
            later = run + _dot3(log_keep, tri)
            a = jnp.exp(log_beta + later)
            if diagonal:
                a = jnp.where(col < row, a, 0.0)
            acc = acc + _dot(a.astype(BF16), v)
            run = run + jnp.sum(log_keep, axis=-1, keepdims=True)
            new.append((run, acc))
            t = jnp.max(run)
            top = t if top is None else jnp.maximum(top, t)
        return tuple(new), top

    init = (jnp.zeros((tq, 1), F32), jnp.zeros((tq, PAIR), F32))
    carries, top = tile(i, (init, init), True)

    def more(state):
        j, top, _ = state
        return jnp.logical_and(j >= 0, top >= EXP_ZERO)

    def body(state):
        j, _, carries = state
        carries, top = tile(j, carries, False)
        return j - 1, top, carries

    _, _, carries = lax.while_loop(more, body, (i - 1, top, carries))
    o_ref[...] = jnp.where(_head_lanes((tq, PAIR), 0), carries[0][1], carries[1][1])


def _sb(main, tq):
    S = main.shape[0]
    tri = (np.arange(tq)[:, None] > np.arange(tq)[None, :]).astype(np.float32)
    return pl.pallas_call(
        functools.partial(_sb_kernel, tq=tq),
        grid=(2, S // tq),
        in_specs=_pair_specs(S, tq, COL_QD, COL_KD, COL_VD) + [_resident((tq, tq), lambda p_, i: (0, 0))],
        out_specs=pl.BlockSpec((tq, PAIR), lambda p_, i: (i, p_)),
        out_shape=jax.ShapeDtypeStruct((S, GROUP_WIDTH), F32),
        compiler_params=_params("parallel", "parallel"),
        name="stick_breaking_attention",
    )(main, main, main, jnp.asarray(tri, BF16))


def _post_kernel(h_ref, oa_ref, ob_ref, oc_ref, od_ref, p_ref, ghead_ref, gmean_ref, wout_ref, gmlp_ref, wup_ref,
                 wdown_ref, gple_ref, wgate_ref, bgate_ref, wproj_ref, gfinal_ref, out_ref, *, f_chunk, final):
    o = jnp.concatenate([oa_ref[...], ob_ref[...], oc_ref[...], od_ref[...]], axis=1)
    o2 = o * o
    hi = o2.astype(BF16)
    lo = (o2 - hi.astype(F32)).astype(BF16)
    ms = _dot(hi, gmean_ref[...]) + _dot(lo, gmean_ref[...])
    on = o * lax.rsqrt(ms + RMS_EPS) * ghead_ref[...]
    h = h_ref[...] + _dot(on.astype(BF16), wout_ref[...])
    u = _rms(h, gmlp_ref[...]).astype(BF16)
    d_ff = wup_ref.shape[1]
    acc = jnp.zeros(h.shape, F32)
    for f in range(0, d_ff, f_chunk):
        hid = jnp.maximum(_dot(u, wup_ref[:, f:f + f_chunk]), 0.0)
        acc = acc + _dot((hid * hid).astype(BF16), wdown_ref[f:f + f_chunk, :])
    h = h + acc
    gate = jax.nn.sigmoid(_dot(_rms(h, gple_ref[...]).astype(BF16), wgate_ref[...]) + bgate_ref[...])
    h = h + _dot(p_ref[...].astype(BF16), wproj_ref[...]) * gate
    out_ref[...] = _rms(h, gfinal_ref[...]) if final else h


def _post(h, mixers, p, ghead, gmean, wout, gmlp, wup, wdown, gple, wgate, bgate, wproj, gfinal, tm, final):
    S, D = h.shape
    row = lambda i: (i, 0)
    fixed = lambda i: (0, 0)
    vec = _resident((1, D), fixed)
    return pl.pallas_call(
        functools.partial(_post_kernel, f_chunk=1024, final=final),
        grid=(S // tm,),
        in_specs=[pl.BlockSpec((tm, D), row)] + [pl.BlockSpec((tm, GROUP_WIDTH), row)] * 4
                 + [pl.BlockSpec((tm, p.shape[1]), row),
                    vec, _resident(gmean.shape, fixed), _resident(wout.shape, fixed),
                    vec, _resident(wup.shape, fixed), _resident(wdown.shape, fixed),
                    vec, _resident(wgate.shape, fixed), vec, _resident(wproj.shape, fixed), vec],
        out_specs=pl.BlockSpec((tm, D), row),
        out_shape=jax.ShapeDtypeStruct((S, D), F32),
        compiler_params=_params("parallel"),
        name="outproj_mlp_ple",
    )(h, *mixers, p, ghead, gmean, wout, gmlp, wup, wdown, gple, wgate, bgate, wproj, gfinal)


def _inproj_columns():
    splits = (256, 64, 64, 64, 64, 64, 64, 12, 256, 256, 256, 256, 256, 256, 4, 256, 256, 256)
    offs = np.concatenate([[0], np.cumsum(splits)])
    seg = lambda k: np.arange(offs[k], offs[k + 1])
    main = [0, 1, 2, 3, 4, 5, 6, 8, 9, 10, 11, 12, 13, 15, 16, 17]
    perm = np.concatenate([seg(k) for k in main] + [seg(7), seg(14)])
    scale = np.ones(perm.shape[0], np.float32)
    for start in (COL_QA, COL_QB, COL_QC, COL_QD):
        scale[start:start + GROUP_WIDTH] = QK_SCALE
    return perm, scale


def _compress_weights(pos_k, w1_k, w2_k, pos_v, w1_v, w2_v):
    half = NSA_CMP_LEN // 2

    def expand_rows(wk, wv):
        z = jnp.zeros_like(wk).reshape(half, HEAD_DIM, -1)
        k_rows = jnp.concatenate([wk.reshape(half, HEAD_DIM, -1), z], axis=2)
        v_rows = jnp.concatenate([z, wv.reshape(half, HEAD_DIM, -1)], axis=2)
        return jnp.concatenate([k_rows, v_rows], axis=1).reshape(half * PAIR, -1)

    n_top = half * HEAD_DIM
    w1 = jnp.stack([expand_rows(w1_k[:n_top], w1_v[:n_top]), expand_rows(w1_k[n_top:], w1_v[n_top:])]).astype(BF16)
    pos = jnp.concatenate([pos_k, pos_v], axis=1)
    pos = jnp.stack([pos[:half].reshape(-1), pos[half:].reshape(-1)])
    zk = jnp.zeros_like(w2_k)
    w2 = jnp.concatenate([jnp.concatenate([w2_k, zk], axis=1), jnp.concatenate([zk, w2_v], axis=1)], axis=0)
    return pos, w1, w2.astype(BF16)


def _nsa_mixer(main, gates, b_gate, cmp_weights, slopes, consts):
    S = main.shape[0]
    n16 = S // NSA_CMP_STRIDE
    x16 = main[:, COL_KVCMP:COL_KVCMP + PAIR].reshape(n16, NSA_CMP_STRIDE * PAIR)
    kvc = _compress(x16, *cmp_weights)
    tq_c = min(NSA_CQ, S)
    tq = min(NSA_TQ, S)
    tk = min(NSA_TK, S)
    o_cmp, selb, flags = _nsa_cmp(main, kvc, consts["cover"], consts["tile_of_block"], tq_c, slopes)
    flags = (flags[:, 0, :S // tk] > 0.0).astype(jnp.int32).reshape(-1)
    rhs = jnp.concatenate([main[:, COL_KVSEL:COL_KVSEL + HEAD_DIM].T, consts["key_features"]], axis=0)
    bgate = jnp.pad(b_gate, (GATE_COL_NSA, N_GATE - GATE_COL_NSA - b_gate.shape[0])).reshape(1, N_GATE)
    return _nsa_attend(flags, main, rhs, selb, o_cmp, gates, bgate, tq, tk, slopes)


def _dilated_mixer(main, slopes):
    S = main.shape[0]
    outs, lses = [], []
    for window, d in DILATED_PATTERNS:
        o, lse = _band(main, d, window // d, slopes, min(BAND_TQ, S // d // 2))
        outs.append(o)
        lses.append(lse)
    return _dil_combine(outs, lses, min(1024, S))


def _fox_mixer(main, gates, b_f):
    S = main.shape[0]
    H = HEADS_PER_MIXER
    f = gates[:, GATE_COL_F:GATE_COL_F + H].T
    f = jnp.concatenate([f, jnp.zeros_like(f)], axis=0)
    b = jnp.concatenate([b_f, jnp.zeros_like(b_f)]).reshape(2 * H, 1)
    c = _logf_cumsum(f, b, min(512, S))[:H].reshape(2, 2, S)
    return _fox(main, c, min(FOX_TILE, S))


def kernel(x, p, g_mix, w_in, b_nsa_gate, b_forget, cmp_pos_k, cmp_w1_k, cmp_w2_k, cmp_pos_v, cmp_w1_v, cmp_w2_v,
           g_head, w_out, g_mlp, w_up, w_down, g_ple, w_ple_gate, b_ple_gate, w_ple_proj, g_final):
    B, S, D = x.shape
    depth = w_in.shape[0]
    slopes = _alibi_slopes()
    slopes_nsa = tuple(float(s) for s in slopes[0::2])
    slopes_dil = tuple(float(s) for s in slopes[1::2])
    perm, colscale = _inproj_columns()
    n_pad = N_MAIN + N_GATE - perm.shape[0]

    n_blk = S // NSA_SEL_LEN
    n_cmp_pad = S // NSA_CMP_STRIDE
    cstart = np.arange(n_cmp_pad) * NSA_CMP_STRIDE
    bstart = np.arange(n_blk) * NSA_SEL_LEN
    cover = (cstart[:, None] < bstart[None, :] + NSA_SEL_LEN) & (cstart[:, None] + NSA_CMP_LEN - 1 >= bstart[None, :])
    tile_of_block = bstart[:, None] // min(NSA_TK, S) == np.arange(128)[None, :]
    pos = np.arange(S)
    key_features = np.zeros((HEAD_DIM + n_blk, S), np.float32)
    key_features[0] = pos // 128
    key_features[1] = pos % 128
    key_features[HEAD_DIM + pos // NSA_SEL_LEN, pos] = 1.0
    consts = {"cover": jnp.asarray(cover, BF16), "tile_of_block": jnp.asarray(tile_of_block, BF16),
              "key_features": jnp.asarray(key_features, BF16)}
    group = np.arange(D) // HEAD_DIM
    gmean = jnp.asarray((group[:, None] == group[None, :]) / HEAD_DIM, BF16)

    outs = []
    for b in range(B):
        h = x[b]
        for i in range(depth):
            w = jnp.pad(w_in[i][:, perm] * colscale, ((0, 0), (0, n_pad))).astype(BF16)
            main, gates = _inproj(h, g_mix[i].reshape(1, D), w, min(512, S))
            cmp_weights = _compress_weights(cmp_pos_k[i], cmp_w1_k[i], cmp_w2_k[i],
                                            cmp_pos_v[i], cmp_w1_v[i], cmp_w2_v[i])
            mixers = (_nsa_mixer(main, gates, b_nsa_gate[i], cmp_weights, slopes_nsa, consts),
                      _dilated_mixer(main, slopes_dil),
                      _fox_mixer(main, gates, b_forget[i]),
                      _sb(main, min(SB_TILE, S)))
            h = _post(h, mixers, p[i, b], g_head[i].reshape(1, D), gmean, w_out[i].astype(BF16),
                      g_mlp[i].reshape(1, D), w_up[i].astype(BF16), w_down[i].astype(BF16),
                      g_ple[i].reshape(1, D), w_ple_gate[i].astype(BF16), b_ple_gate[i].reshape(1, D),
                      w_ple_proj[i].astype(BF16), g_final.reshape(1, D), min(256, S), i == depth - 1)
        outs.append(h)
    return jnp.stack(outs)
```

```python
import functools
import math

import numpy as np
import jax
import jax.numpy as jnp
from jax import lax
from jax.experimental import pallas as pl
from jax.experimental.pallas import tpu as pltpu

F32 = jnp.float32
BF16 = jnp.bfloat16

HEAD_DIM = 64
HEADS_PER_MIXER = 4
GROUP_WIDTH = HEADS_PER_MIXER * HEAD_DIM
PAIR = 2 * HEAD_DIM
N_ALIBI_HEADS = 2 * HEADS_PER_MIXER
NSA_CMP_LEN = 32
NSA_CMP_STRIDE = 16
NSA_CMP_HIDDEN = 256
NSA_SEL_LEN = 64
NSA_N_SEL = 16
NSA_WINDOW = 512
DILATED_PATTERNS = ((128, 1), (512, 4), (2048, 16))
RMS_EPS = 1e-6
SEL_FORCE = 1e9
QK_SCALE = HEAD_DIM ** -0.5

N_MAIN = 2944
N_GATE = 128
COL_QA = 0
COL_KVCMP, COL_KVSEL, COL_KVWIN = 256, 384, 512
COL_QB, COL_KB, COL_VB = 640, 896, 1152
COL_QC, COL_KC, COL_VC = 1408, 1664, 1920
COL_QD, COL_KD, COL_VD = 2176, 2432, 2688
GATE_COL_NSA, GATE_COL_F = 0, 12

NSA_CQ = 256
NSA_TQ, NSA_TK = 128, 512
FOX_TQ, FOX_TK = 512, 512
SB_TQ, SB_TK = 512, 256
BAND_TQ = 512
EXP_ZERO = -104.0

MASKED = -1e9
M_INIT = -1e8
VMEM_LIMIT = 56 * 1024 * 1024


def _alibi_slopes():
    return 2.0 ** (-8.0 * np.arange(1, N_ALIBI_HEADS + 1) / N_ALIBI_HEADS)


def _params(*sem):
    return pltpu.CompilerParams(dimension_semantics=sem, vmem_limit_bytes=VMEM_LIMIT)


def _resident(shape, index_map):
    return pl.BlockSpec(shape, index_map, pipeline_mode=pl.Buffered(1))


def _dot(a, b):
    return jnp.dot(a, b, preferred_element_type=F32)


def _dot_nt(a, b):
    return lax.dot_general(a, b, (((1,), (1,)), ((), ())), preferred_element_type=F32)


def _split3(x):
    hi = x.astype(BF16)
    r = x - hi.astype(F32)
    mid = r.astype(BF16)
    lo = (r - mid.astype(F32)).astype(BF16)
    return hi, mid, lo


def _dot3(x, w01):
    hi, mid, lo = _split3(x)
    return _dot(hi, w01) + _dot(mid, w01) + _dot(lo, w01)


def _rms(x, g):
    return x * lax.rsqrt(jnp.mean(x * x, axis=-1, keepdims=True) + RMS_EPS) * g


def _softplus(z):
    return jnp.maximum(z, 0.0) + jnp.log1p(jnp.exp(-jnp.abs(z)))


def _head_lanes(shape, hh):
    lane = lax.broadcasted_iota(jnp.int32, shape, len(shape) - 1)
    return lane < HEAD_DIM if hh == 0 else lane >= HEAD_DIM


def _inproj_kernel(h_ref, g_ref, w_ref, main_ref, gate_ref, x16_ref, xd4_ref, xd16_ref, rows_ref):
    tm = h_ref.shape[0]
    u = _rms(h_ref[...], g_ref[...]).astype(BF16)
    acc = _dot(u, w_ref[...])
    main_ref[...] = acc[:, :N_MAIN].astype(main_ref.dtype)
    gate_ref[...] = acc[:, N_MAIN:]
    n_pairs = 3 * GROUP_WIDTH // PAIR
    rows_ref[0] = acc[:, COL_KVCMP:COL_KVCMP + PAIR]
    for c in range(n_pairs):
        rows_ref[1 + c] = acc[:, COL_QB + c * PAIR:COL_QB + (c + 1) * PAIR]
    n16 = tm // NSA_CMP_STRIDE
    for l in range(NSA_CMP_STRIDE):
        x16_ref[:, l * PAIR:(l + 1) * PAIR] = rows_ref[0, pl.ds(l, n16, stride=NSA_CMP_STRIDE), :].astype(BF16)
    for out_ref in (xd4_ref, xd16_ref):
        d = out_ref.shape[0]
        for r in range(d):
            for c in range(n_pairs):
                out_ref[r, :, c * PAIR:(c + 1) * PAIR] = rows_ref[1 + c, pl.ds(r, tm // d, stride=d), :].astype(BF16)


def _inproj(h, g, w, tm):
    S, D = h.shape
    n = w.shape[1]
    d4, d16 = DILATED_PATTERNS[1][1], DILATED_PATTERNS[2][1]
    wide = 3 * GROUP_WIDTH
    return pl.pallas_call(
        _inproj_kernel,
        grid=(S // tm,),
        in_specs=[pl.BlockSpec((tm, D), lambda i: (i, 0)),
                  _resident((1, D), lambda i: (0, 0)),
                  _resident((D, n), lambda i: (0, 0))],
        out_specs=[pl.BlockSpec((tm, N_MAIN), lambda i: (i, 0)),
                   pl.BlockSpec((tm, N_GATE), lambda i: (i, 0)),
                   pl.BlockSpec((tm // NSA_CMP_STRIDE, NSA_CMP_STRIDE * PAIR), lambda i: (i, 0)),
                   pl.BlockSpec((d4, tm // d4, wide), lambda i: (0, i, 0)),
                   pl.BlockSpec((d16, tm // d16, wide), lambda i: (0, i, 0))],
        out_shape=[jax.ShapeDtypeStruct((S, N_MAIN), BF16), jax.ShapeDtypeStruct((S, N_GATE), F32),
                   jax.ShapeDtypeStruct((S // NSA_CMP_STRIDE, NSA_CMP_STRIDE * PAIR), BF16),
                   jax.ShapeDtypeStruct((d4, S // d4, wide), BF16),
                   jax.ShapeDtypeStruct((d16, S // d16, wide), BF16)],
        scratch_shapes=[pltpu.VMEM((1 + wide // PAIR, tm, PAIR), F32)],
        compiler_params=_params("parallel"),
        name="inproj",
    )(h, g, w)


def _compress_kernel(x_ref, pos_ref, w1_ref, w2_ref, out_ref):
    x = x_ref[...].astype(F32)
    n = x.shape[0]
    top = _dot((x + pos_ref[0:1, :]).astype(BF16), w1_ref[0])
    bot = _dot((x + pos_ref[1:2, :]).astype(BF16), w1_ref[1])
    hid = top + pltpu.roll(bot, n - 1, 0)
    c = math.sqrt(2.0 / math.pi)
    hid = 0.5 * hid * (1.0 + jnp.tanh(c * (hid + 0.044715 * (hid * hid * hid))))
    out_ref[...] = _dot(hid.astype(BF16), w2_ref[...]).astype(out_ref.dtype)


def _compress(x16, pos, w1, w2):
    n = x16.shape[0]
    return pl.pallas_call(
        _compress_kernel,
        out_shape=jax.ShapeDtypeStruct((n, PAIR), BF16),
        compiler_params=pltpu.CompilerParams(vmem_limit_bytes=VMEM_LIMIT),
        name="nsa_compress",
    )(x16, pos, w1, w2)


def _nsa_cmp_kernel(q_ref, kvc_ref, cover_ref, tile_ref, oc_ref, selb_ref, flag_ref, *, tq, slopes):
    t0 = pl.program_id(0) * tq
    nc = kvc_ref.shape[0]
    nb = cover_ref.shape[1]
    tpos = t0 + lax.broadcasted_iota(jnp.int32, (tq, nc), 0)
    jidx = lax.broadcasted_iota(jnp.int32, (tq, nc), 1)
    mask = jidx * NSA_CMP_STRIDE + (NSA_CMP_LEN - 1) <= tpos
    dist = tpos.astype(F32) - (jidx.astype(F32) * NSA_CMP_STRIDE + (NSA_CMP_LEN - 1) / 2.0)
    kvc = kvc_ref[...]
    q = q_ref[...]
    zeros = jnp.zeros((tq, HEAD_DIM), BF16)
    psum = jnp.zeros((tq, nc), F32)
    outs = []
    for h in range(HEADS_PER_MIXER):
        qh = jnp.concatenate([q[:, h * HEAD_DIM:(h + 1) * HEAD_DIM], zeros], axis=1)
        s = _dot_nt(qh, kvc) - slopes[h] * dist
        m = jnp.max(jnp.where(mask, s, -jnp.inf), axis=-1, keepdims=True)
        m = jnp.where(m > -jnp.inf, m, 0.0)
        e = jnp.where(mask, jnp.exp(s - m), 0.0)
        den = jnp.sum(e, axis=-1, keepdims=True)
        p = e / jnp.maximum(den, 1e-30)
        outs.append(_dot(p.astype(BF16), kvc)[:, HEAD_DIM:])
        psum = psum + p
    oc_ref[...] = jnp.concatenate(outs, axis=1)
    imp = _dot3(psum, cover_ref[...])
    bidx = lax.broadcasted_iota(jnp.int32, (tq, nb), 1)
    qpos = t0 + lax.broadcasted_iota(jnp.int32, (tq, nb), 0)
    cur = qpos // NSA_SEL_LEN
    forced = jnp.where(bidx == 0, 1.0, jnp.where(bidx == cur, 1.0, jnp.where(bidx == cur - 1, 1.0, 0.0)))
    elig = bidx * NSA_SEL_LEN <= qpos
    score = jnp.where(forced > 0.0, SEL_FORCE, jnp.where(elig, imp, -SEL_FORCE))
    bf = bidx.astype(F32)
    sel = jnp.zeros((tq, nb), F32)
    for _ in range(min(NSA_N_SEL, nb)):
        mx = jnp.max(score, axis=-1, keepdims=True)
        first = jnp.min(jnp.where(score == mx, bf, float(nb)), axis=-1, keepdims=True)
        hit = bf == first
        sel = jnp.where(hit, 1.0, sel)
        score = jnp.where(hit, -jnp.inf, score)
    sel = jnp.where(elig, sel, 0.0)
    selb_ref[...] = jnp.where(sel > 0.0, 0.0, MASKED).astype(selb_ref.dtype)
    for part in range(tq // NSA_TQ):
        any_q = jnp.max(sel[part * NSA_TQ:(part + 1) * NSA_TQ], axis=0, keepdims=True).astype(BF16)
        flag_ref[part] = _dot(jnp.broadcast_to(any_q, (8, nb)), tile_ref[...])


def _nsa_cmp(main, kvc, cover, tile_of_block, tq, slopes):
    S = main.shape[0]
    nc = kvc.shape[0]
    nb = cover.shape[1]
    parts = tq // NSA_TQ
    return pl.pallas_call(
        functools.partial(_nsa_cmp_kernel, tq=tq, slopes=slopes),
        grid=(S // tq,),
        in_specs=[pl.BlockSpec((tq, GROUP_WIDTH), lambda i: (i, COL_QA // GROUP_WIDTH)),
                  _resident((nc, PAIR), lambda i: (0, 0)),
                  _resident((nc, nb), lambda i: (0, 0)),
                  _resident((nb, 128), lambda i: (0, 0))],
        out_specs=[pl.BlockSpec((tq, GROUP_WIDTH), lambda i: (i, 0)),
                   pl.BlockSpec((tq, nb), lambda i: (i, 0)),
                   pl.BlockSpec((parts, 8, 128), lambda i: (i, 0, 0))],
        out_shape=[jax.ShapeDtypeStruct((S, GROUP_WIDTH), F32), jax.ShapeDtypeStruct((S, nb), BF16),
                   jax.ShapeDtypeStruct((S // NSA_TQ, 8, 128), F32)],
        compiler_params=_params("parallel"),
        name="nsa_cmp_select",
    )(main, kvc, cover, tile_of_block)


def _nsa_attend_kernel(flag_ref, q_ref, rhs_ref, kvsel_ref, kvwin_ref, selb_ref, ocmp_ref, gate_ref, bgate_ref,
                       o_ref, lhs_ref, m_ref, l_ref, acc_ref, *, tq, tk, slopes):
    H = HEADS_PER_MIXER
    S = kvsel_ref.shape[0]
    i = pl.program_id(0)
    t0 = i * tq
    flag_base = i * (S // tk)
    rows = H * tq
    nb = selb_ref.shape[1]

    q = q_ref[...]
    selb = selb_ref[...]
    lane = lax.broadcasted_iota(jnp.int32, (tq, HEAD_DIM), 1)
    for h in range(H):
        aug = jnp.where(lane == 0, 128.0 * slopes[h], jnp.where(lane == 1, slopes[h], 0.0)).astype(BF16)
        lhs_ref[h * tq:(h + 1) * tq, 0:PAIR] = jnp.concatenate([q[:, h * HEAD_DIM:(h + 1) * HEAD_DIM], aug], axis=1)
        lhs_ref[h * tq:(h + 1) * tq, PAIR:PAIR + nb] = selb
    m_ref[...] = jnp.full(m_ref.shape, M_INIT, F32)
    l_ref[...] = jnp.zeros(l_ref.shape, F32)
    acc_ref[...] = jnp.zeros(acc_ref.shape, F32)

    def tile(j, diagonal):
        ks = pl.multiple_of(j * tk, tk)
        rhs = rhs_ref[:, pl.ds(ks, tk)]
        kv = kvsel_ref[pl.ds(ks, tk), :]
        half = rows // 2
        for rs in (slice(0, half), slice(half, rows)):
            s = _dot(lhs_ref[rs, :], rhs)
            if diagonal:
                qpos = jnp.concatenate([t0 + lax.broadcasted_iota(jnp.int32, (tq, tk), 0)] * (H // 2), axis=0)
                kpos = ks + lax.broadcasted_iota(jnp.int32, (half, tk), 1)
                s = jnp.where(kpos <= qpos, s, MASKED)
            m_old = m_ref[rs, :]
            m_new = jnp.maximum(m_old, jnp.max(s, axis=-1, keepdims=True))
            p = jnp.exp(s - m_new)
            alpha = jnp.exp(m_old - m_new)
            l_ref[rs, :] = alpha * l_ref[rs, :] + jnp.sum(p, axis=-1, keepdims=True)
            acc_ref[rs, :] = alpha * acc_ref[rs, :] + _dot(p.astype(BF16), kv)
            m_ref[rs, :] = m_new

    last = (t0 + tq - 1) // tk

    def step(j, _):
        @pl.when(flag_ref[flag_base + j] > 0)
        def _():
            tile(j, False)
        return 0

    lax.fori_loop(0, last, step, 0)
    tile(last, True)
    o_sel = acc_ref[:, HEAD_DIM:] / jnp.maximum(l_ref[...], 1e-30)

    back = NSA_WINDOW - 1
    width = NSA_WINDOW + tq
    start = pl.multiple_of(jnp.maximum(t0 - NSA_WINDOW, 0), tq)
    kw = kvwin_ref[pl.ds(start, width), :]
    qz = jnp.where(_head_lanes((rows, PAIR), 0), lhs_ref[:, 0:PAIR], jnp.zeros((rows, PAIR), BF16))
    sw = _dot_nt(qz, kw)
    qpos = jnp.concatenate([t0 + lax.broadcasted_iota(jnp.int32, (tq, width), 0)] * H, axis=0)
    dist = qpos - (start + lax.broadcasted_iota(jnp.int32, (rows, width), 1))
    slope = jnp.concatenate([jnp.full((tq, 1), slopes[h], F32) for h in range(H)], axis=0)
    sw = sw - slope * dist.astype(F32)
    sw = jnp.where(dist >= 0, jnp.where(dist <= back, sw, -jnp.inf), -jnp.inf)
    mw = jnp.max(sw, axis=-1, keepdims=True)
    ew = jnp.exp(sw - mw)
    o_win = _dot(ew.astype(BF16), kw)[:, HEAD_DIM:] / jnp.sum(ew, axis=-1, keepdims=True)

    g = jax.nn.sigmoid(gate_ref[...] + bgate_ref[...])
    ocmp = ocmp_ref[...]
    outs = []
    for h in range(H):
        r0 = h * tq
        c0 = GATE_COL_NSA + h
        outs.append(g[:, c0:c0 + 1] * ocmp[:, h * HEAD_DIM:(h + 1) * HEAD_DIM]
                    + g[:, c0 + H:c0 + H + 1] * o_sel[r0:r0 + tq]
                    + g[:, c0 + 2 * H:c0 + 2 * H + 1] * o_win[r0:r0 + tq])
    o_ref[...] = _head_norm(jnp.concatenate(outs, axis=1))


def _nsa_attend(flags, main, rhs, selb, ocmp, gates, bgate, tq, tk, slopes):
    S = main.shape[0]
    nb = selb.shape[1]
    rows = HEADS_PER_MIXER * tq
    grid_spec = pltpu.PrefetchScalarGridSpec(
        num_scalar_prefetch=1,
        grid=(S // tq,),
        in_specs=[pl.BlockSpec((tq, GROUP_WIDTH), lambda i, fl: (i, COL_QA // GROUP_WIDTH)),
                  _resident((rhs.shape[0], S), lambda i, fl: (0, 0)),
                  _resident((S, PAIR), lambda i, fl: (0, COL_KVSEL // PAIR)),
                  _resident((S, PAIR), lambda i, fl: (0, COL_KVWIN // PAIR)),
                  pl.BlockSpec((tq, nb), lambda i, fl: (i, 0)),
                  pl.BlockSpec((tq, GROUP_WIDTH), lambda i, fl: (i, 0)),
                  pl.BlockSpec((tq, N_GATE), lambda i, fl: (i, 0)),
                  _resident((1, N_GATE), lambda i, fl: (0, 0))],
        out_specs=pl.BlockSpec((tq, GROUP_WIDTH), lambda i, fl: (i, 0)),
        scratch_shapes=[pltpu.VMEM((rows, PAIR + nb), BF16), pltpu.VMEM((rows, 1), F32),
                        pltpu.VMEM((rows, 1), F32), pltpu.VMEM((rows, PAIR), F32)],
    )
    return pl.pallas_call(
        functools.partial(_nsa_attend_kernel, tq=tq, tk=tk, slopes=slopes),
        grid_spec=grid_spec,
        out_shape=jax.ShapeDtypeStruct((S, GROUP_WIDTH), F32),
        compiler_params=_params("parallel"),
        name="nsa_attend",
    )(flags, main, rhs, main, main, selb, ocmp, gates, bgate)


def _band_kernel(q_ref, k_ref, v_ref, dist_ref, o_ref, lse_ref, *, tq, back_pad, slopes, d):
    pr = pl.program_id(1)
    q0 = pl.multiple_of(pl.program_id(2) * tq, tq)
    width = back_pad + tq
    start = pl.multiple_of(jnp.maximum(q0 - back_pad, 0), 128)
    k = k_ref[0, pl.ds(start, width), :]
    v = v_ref[0, pl.ds(start, width), :]
    q = q_ref[0]
    dist = dist_ref[jnp.where(q0 < back_pad, 0, 1)]
    o_pair, lse_pair = [], []
    for hh in range(2):
        slope = jnp.where(pr == 0, slopes[hh], slopes[2 + hh]) * d
        qh = jnp.where(_head_lanes(q.shape, hh), q, jnp.zeros_like(q))
        s = _dot_nt(qh, k) - slope * dist
        m = jnp.max(s, axis=-1, keepdims=True)
        e = jnp.exp(s - m)
        l = jnp.sum(e, axis=-1, keepdims=True)
        o_pair.append(_dot(e.astype(BF16), v) / l)
        lse_pair.append(m + jnp.log(l))
    first = _head_lanes((tq, PAIR), 0)
    o_ref[0] = jnp.where(first, o_pair[0], o_pair[1])
    lse_ref[0] = jnp.where(first, lse_pair[0], lse_pair[1])


def _band(src, cols, d, back, slopes, tq):
    _, L, _ = src.shape
    back_pad = -(-back // 128) * 128
    width = back_pad + tq
    r_ = np.arange(tq)[:, None]
    c_ = np.arange(width)[None, :]
    dist = np.stack([r_ - c_, r_ + back_pad - c_]).astype(np.float32)
    dist = np.where((dist >= 0) & (dist <= back), dist, 1e30)
    cq, ck, cv = cols
    out_spec = pl.BlockSpec((1, tq, PAIR), lambda r, p_, i: (r, i, p_))
    return pl.pallas_call(
        functools.partial(_band_kernel, tq=tq, back_pad=back_pad, slopes=slopes, d=float(d)),
        grid=(d, 2, L // tq),
        in_specs=[pl.BlockSpec((1, tq, PAIR), lambda r, p_, i: (r, i, cq + p_)),
                  pl.BlockSpec((1, L, PAIR), lambda r, p_, i: (r, 0, ck + p_)),
                  pl.BlockSpec((1, L, PAIR), lambda r, p_, i: (r, 0, cv + p_)),
                  _resident((2, tq, width), lambda r, p_, i: (0, 0, 0))],
        out_specs=[out_spec, out_spec],
        out_shape=[jax.ShapeDtypeStruct((d, L, GROUP_WIDTH), F32)] * 2,
        compiler_params=_params("parallel", "parallel", "parallel"),
        name="band_attention",
    )(src, src, src, jnp.asarray(dist, F32))


def _head_norm(o):
    lane = lax.broadcasted_iota(jnp.int32, o.shape, 1)
    o2 = o * o
    scale = jnp.zeros_like(o)
    for h in range(o.shape[1] // HEAD_DIM):
        mine = jnp.logical_and(lane >= h * HEAD_DIM, lane < (h + 1) * HEAD_DIM)
        ms = jnp.sum(jnp.where(mine, o2, 0.0), axis=-1, keepdims=True) * (1.0 / HEAD_DIM)
        scale = jnp.where(mine, lax.rsqrt(ms + RMS_EPS), scale)
    return o * scale


def _dil_combine_kernel(o1_ref, l1_ref, o4_ref, l4_ref, o16_ref, l16_ref, out_ref, o4_s, l4_s, o16_s, l16_s):
    tm, width = out_ref.shape
    n_pairs = width // PAIR
    for src, dst in ((o4_ref, o4_s), (l4_ref, l4_s), (o16_ref, o16_s), (l16_ref, l16_s)):
        d = src.shape[0]
        for r in range(d):
            for c in range(n_pairs):
                dst[c, pl.ds(r, tm // d, stride=d), :] = src[r, :, c * PAIR:(c + 1) * PAIR]
    token_order = lambda s: jnp.concatenate([s[c] for c in range(n_pairs)], axis=1)
    l1, l2, l3 = l1_ref[0], token_order(l4_s), token_order(l16_s)
    m = jnp.maximum(l1, jnp.maximum(l2, l3))
    e1, e2, e3 = jnp.exp(l1 - m), jnp.exp(l2 - m), jnp.exp(l3 - m)
    den = e1 + e2 + e3
    out_ref[...] = _head_norm((e1 / den) * o1_ref[0] + (e2 / den) * token_order(o4_s)
                              + (e3 / den) * token_order(o16_s))


def _dil_combine(outs, lses, tm):
    S, W = outs[0].shape[1:]
    specs = []
    for o in outs:
        d = o.shape[0]
        specs += [pl.BlockSpec((d, tm // d, W), lambda i: (0, i, 0))] * 2
    args = [a for pair in zip(outs, lses) for a in pair]
    return pl.pallas_call(
        _dil_combine_kernel,
        grid=(S // tm,),
        in_specs=specs,
        out_specs=pl.BlockSpec((tm, W), lambda i: (i, 0)),
        out_shape=jax.ShapeDtypeStruct((S, W), F32),
        scratch_shapes=[pltpu.VMEM((W // PAIR, tm, PAIR), F32)] * 4,
        compiler_params=_params("parallel"),
        name="dilated_combine",
    )(*args)


def _logf_cumsum_kernel(f_ref, b_ref, tri_ref, c_ref, carry_ref):
    @pl.when(pl.program_id(0) == 0)
    def _():
        carry_ref[...] = jnp.zeros_like(carry_ref)

    x = f_ref[...] + b_ref[...]
    logf = jnp.minimum(x, 0.0) - jnp.log1p(jnp.exp(-jnp.abs(x)))
    c = _dot3(logf, tri_ref[...]) + carry_ref[:, :1]
    c_ref[...] = c
    carry_ref[...] = jnp.broadcast_to(c[:, -1:], carry_ref.shape)


def _logf_cumsum(f, b, tk):
    R, S = f.shape
    tri = (np.arange(tk)[:, None] <= np.arange(tk)[None, :]).astype(np.float32)
    return pl.pallas_call(
        _logf_cumsum_kernel,
        grid=(S // tk,),
        in_specs=[pl.BlockSpec((R, tk), lambda i: (0, i)),
                  pl.BlockSpec((R, 1), lambda i: (0, 0)),
                  _resident((tk, tk), lambda i: (0, 0))],
        out_specs=pl.BlockSpec((R, tk), lambda i: (0, i)),
        out_shape=jax.ShapeDtypeStruct((R, S), F32),
        scratch_shapes=[pltpu.VMEM((R, 128), F32)],
        compiler_params=_params("arbitrary"),
        name="logf_cumsum",
    )(f, b, jnp.asarray(tri, BF16))


def _pair_key_norms(k_ref, kmax_ref, chunk):
    S = k_ref.shape[0]
    first = _head_lanes((chunk, PAIR), 0)

    def body(ci, best):
        k = k_ref[pl.ds(pl.multiple_of(ci * chunk, chunk), chunk), :].astype(F32)
        sq = k * k
        n0 = jnp.max(jnp.sum(jnp.where(first, sq, 0.0), axis=-1, keepdims=True), axis=0, keepdims=True)
        n1 = jnp.max(jnp.sum(jnp.where(first, 0.0, sq), axis=-1, keepdims=True), axis=0, keepdims=True)
        return jnp.maximum(best[0], n0), jnp.maximum(best[1], n1)

    zero = jnp.zeros((1, 1), F32)
    n0, n1 = lax.fori_loop(0, S // chunk, body, (zero, zero))
    kmax_ref[0:1, :] = jnp.broadcast_to(jnp.sqrt(n0), (1, kmax_ref.shape[1]))
    kmax_ref[1:2, :] = jnp.broadcast_to(jnp.sqrt(n1), (1, kmax_ref.shape[1]))


def _fox_kernel(q_ref, k_ref, v_ref, c_ref, o_ref, kmax_ref, *, tq, tk):
    i = pl.program_id(1)
    first = (i * tq) // tk

    @pl.when(i == 0)
    def _():
        _pair_key_norms(k_ref, kmax_ref, min(1024, k_ref.shape[0]))

    q = q_ref[...]
    row = lax.broadcasted_iota(jnp.int32, (tq, tk), 0) + (i * tq - first * tk)
    col = lax.broadcasted_iota(jnp.int32, (tq, tk), 1)
    qs, reach = [], []
    for hh in range(2):
        qh = jnp.where(_head_lanes(q.shape, hh), q, jnp.zeros_like(q))
        qf = qh.astype(F32)
        qs.append(qh)
        reach.append(jnp.sqrt(jnp.sum(qf * qf, axis=-1, keepdims=True)) * kmax_ref[hh:hh + 1, :1])

    def tile(j, carries, diagonal):
        ks = pl.multiple_of(j * tk, tk)
        k = k_ref[pl.ds(ks, tk), :]
        v = v_ref[pl.ds(ks, tk), :]
        new, gap = [], None
        for hh in range(2):
            m, l, acc = carries[hh]
            c_tile = c_ref[0, hh:hh + 1, pl.ds(ks, tk)]
            s = _dot_nt(qs[hh], k) - c_tile
            if diagonal:
                s = jnp.where(col <= row, s, -jnp.inf)
            m_new = jnp.maximum(m, jnp.max(s, axis=-1, keepdims=True))
            p = jnp.exp(s - m_new)
            alpha = jnp.exp(m - m_new)
            l = alpha * l + jnp.sum(p, axis=-1, keepdims=True)
            acc = alpha * acc + _dot(p.astype(BF16), v)
            new.append((m_new, l, acc))
            g = jnp.max(reach[hh] - c_tile[:, :1] - m_new)
            gap = g if gap is None else jnp.maximum(gap, g)
        return tuple(new), gap

    init = (jnp.full((tq, 1), -jnp.inf, F32), jnp.zeros((tq, 1), F32), jnp.zeros((tq, PAIR), F32))
    carries, gap = tile(first, (init, init), True)

    def more(state):
        j, gap, _ = state
        return jnp.logical_and(j >= 0, gap >= EXP_ZERO - 1.0)

    def body(state):
        j, _, carries = state
        carries, gap = tile(j, carries, False)
        return j - 1, gap, carries

    _, _, carries = lax.while_loop(more, body, (first - 1, gap, carries))
    outs = [acc / l for _, l, acc in carries]
    o_ref[...] = _head_norm(jnp.where(_head_lanes((tq, PAIR), 0), outs[0], outs[1]))


def _pair_specs(S, tq, col_q, col_k, col_v):
    return [pl.BlockSpec((tq, PAIR), lambda p_, i: (i, col_q // PAIR + p_)),
            pl.BlockSpec((S, PAIR), lambda p_, i: (0, col_k // PAIR + p_)),
            pl.BlockSpec((S, PAIR), lambda p_, i: (0, col_v // PAIR + p_))]


def _fox(main, c, tq, tk):
    S = main.shape[0]
    return pl.pallas_call(
        functools.partial(_fox_kernel, tq=tq, tk=tk),
        grid=(2, S // tq),
        in_specs=_pair_specs(S, tq, COL_QC, COL_KC, COL_VC) + [pl.BlockSpec((1, 2, S), lambda p_, i: (p_, 0, 0))],
        out_specs=pl.BlockSpec((tq, PAIR), lambda p_, i: (i, p_)),
        out_shape=jax.ShapeDtypeStruct((S, GROUP_WIDTH), F32),
        scratch_shapes=[pltpu.VMEM((8, 128), F32)],
        compiler_params=_params("parallel", "arbitrary"),
        name="forgetting_attention",
    )(main, main, main, c)


def _sb_kernel(q_ref, k_ref, v_ref, tri_ref, o_ref, *, tq, tk):
    i = pl.program_id(1)
    per_q = tq // tk
    q = q_ref[...]
    tri = tri_ref[...]
    row = lax.broadcasted_iota(jnp.int32, (tq, tk), 0)
    col = lax.broadcasted_iota(jnp.int32, (tq, tk), 1)
    qs = [jnp.where(_head_lanes(q.shape, hh), q, jnp.zeros_like(q)) for hh in range(2)]

    def tile(j, carries, lead):
        ks = pl.multiple_of(j * tk, tk)
        k = k_ref[pl.ds(ks, tk), :]
        v = v_ref[pl.ds(ks, tk), :]
        strict = None if lead is None else col + lead * tk < row
        new, top = [], None
        for hh in range(2):
            run, acc = carries[hh]
            z = _dot_nt(qs[hh], k)
            log_keep = -_softplus(z)
            log_beta = z + log_keep
            if strict is not None:
                log_keep = jnp.where(strict, log_keep, 0.0)
            later = run + _dot3(log_keep, tri)
            a = jnp.exp(log_beta + later)
            if strict is not None:
                a = jnp.where(strict, a, 0.0)
            acc = acc + _dot(a.astype(BF16), v)
            run = run + jnp.sum(log_keep, axis=-1, keepdims=True)
            new.append((run, acc))
            t = jnp.max(run)
            top = t if top is None else jnp.maximum(top, t)
        return tuple(new), top

    init = (jnp.zeros((tq, 1), F32), jnp.zeros((tq, PAIR), F32))
    carries, top = (init, init), None
    for lead in reversed(range(per_q)):
        carries, top = tile(i * per_q + lead, carries, lead)

    def more(state):
        j, top, _ = state
        return jnp.logical_and(j >= 0, top >= EXP_ZERO)

    def body(state):
        j, _, carries = state
        carries, top = tile(j, carries, None)
        return j - 1, top, carries

    _, _, carries = lax.while_loop(more, body, (i * per_q - 1, top, carries))
    o_ref[...] = _head_norm(jnp.where(_head_lanes((tq, PAIR), 0), carries[0][1], carries[1][1]))


def _sb(main, tq, tk):
    S = main.shape[0]
    tri = (np.arange(tk)[:, None] > np.arange(tk)[None, :]).astype(np.float32)
    return pl.pallas_call(
        functools.partial(_sb_kernel, tq=tq, tk=tk),
        grid=(2, S // tq),
        in_specs=_pair_specs(S, tq, COL_QD, COL_KD, COL_VD) + [_resident((tk, tk), lambda p_, i: (0, 0))],
        out_specs=pl.BlockSpec((tq, PAIR), lambda p_, i: (i, p_)),
        out_shape=jax.ShapeDtypeStruct((S, GROUP_WIDTH), F32),
        compiler_params=_params("parallel", "parallel"),
        name="stick_breaking_attention",
    )(main, main, main, jnp.asarray(tri, BF16))


def _post_kernel(h_ref, oa_ref, ob_ref, oc_ref, od_ref, p_ref, ghead_ref, wout_ref, gmlp_ref, wup_ref,
                 wdown_ref, gple_ref, wgate_ref, bgate_ref, wproj_ref, gfinal_ref, out_ref, *, f_chunk, final):
    o = jnp.concatenate([oa_ref[...], ob_ref[...], oc_ref[...], od_ref[...]], axis=1)
    on = o * ghead_ref[...]
    h = h_ref[...] + _dot(on.astype(BF16), wout_ref[...])
    u = _rms(h, gmlp_ref[...]).astype(BF16)
    d_ff = wup_ref.shape[1]
    acc = jnp.zeros(h.shape, F32)
    for f in range(0, d_ff, f_chunk):
        hid = jnp.maximum(_dot(u, wup_ref[:, f:f + f_chunk]), 0.0)
        acc = acc + _dot((hid * hid).astype(BF16), wdown_ref[f:f + f_chunk, :])
    h = h + acc
    gate = jax.nn.sigmoid(_dot(_rms(h, gple_ref[...]).astype(BF16), wgate_ref[...]) + bgate_ref[...])
    h = h + _dot(p_ref[...].astype(BF16), wproj_ref[...]) * gate
    out_ref[...] = _rms(h, gfinal_ref[...]) if final else h


def _post(h, mixers, p, ghead, wout, gmlp, wup, wdown, gple, wgate, bgate, wproj, gfinal, tm, final):
    S, D = h.shape
    row = lambda i: (i, 0)
    fixed = lambda i: (0, 0)
    vec = _resident((1, D), fixed)
    return pl.pallas_call(
        functools.partial(_post_kernel, f_chunk=1024, final=final),
        grid=(S // tm,),
        in_specs=[pl.BlockSpec((tm, D), row)] + [pl.BlockSpec((tm, GROUP_WIDTH), row)] * 4
                 + [pl.BlockSpec((tm, p.shape[1]), row),
                    vec, _resident(wout.shape, fixed),
                    vec, _resident(wup.shape, fixed), _resident(wdown.shape, fixed),
                    vec, _resident(wgate.shape, fixed), vec, _resident(wproj.shape, fixed), vec],
        out_specs=pl.BlockSpec((tm, D), row),
        out_shape=jax.ShapeDtypeStruct((S, D), F32),
        compiler_params=_params("parallel"),
        name="outproj_mlp_ple",
    )(h, *mixers, p, ghead, wout, gmlp, wup, wdown, gple, wgate, bgate, wproj, gfinal)


def _inproj_columns():
    splits = (256, 64, 64, 64, 64, 64, 64, 12, 256, 256, 256, 256, 256, 256, 4, 256, 256, 256)
    offs = np.concatenate([[0], np.cumsum(splits)])
    seg = lambda k: np.arange(offs[k], offs[k + 1])
    main = [0, 1, 2, 3, 4, 5, 6, 8, 9, 10, 11, 12, 13, 15, 16, 17]
    perm = np.concatenate([seg(k) for k in main] + [seg(7), seg(14)])
    scale = np.ones(perm.shape[0], np.float32)
    for start in (COL_QA, COL_QB, COL_QC, COL_QD):
        scale[start:start + GROUP_WIDTH] = QK_SCALE
    return perm, scale


def _compress_weights(pos_k, w1_k, w2_k, pos_v, w1_v, w2_v):
    half = NSA_CMP_LEN // 2

    def expand_rows(wk, wv):
        z = jnp.zeros_like(wk).reshape(half, HEAD_DIM, -1)
        k_rows = jnp.concatenate([wk.reshape(half, HEAD_DIM, -1), z], axis=2)
        v_rows = jnp.concatenate([z, wv.reshape(half, HEAD_DIM, -1)], axis=2)
        return jnp.concatenate([k_rows, v_rows], axis=1).reshape(half * PAIR, -1)

    n_top = half * HEAD_DIM
    w1 = jnp.stack([expand_rows(w1_k[:n_top], w1_v[:n_top]), expand_rows(w1_k[n_top:], w1_v[n_top:])]).astype(BF16)
    pos = jnp.concatenate([pos_k, pos_v], axis=1)
    pos = jnp.stack([pos[:half].reshape(-1), pos[half:].reshape(-1)])
    zk = jnp.zeros_like(w2_k)
    w2 = jnp.concatenate([jnp.concatenate([w2_k, zk], axis=1), jnp.concatenate([zk, w2_v], axis=1)], axis=0)
    return pos, w1, w2.astype(BF16)


def _nsa_mixer(main, x16, gates, b_gate, cmp_weights, slopes, consts):
    S = main.shape[0]
    kvc = _compress(x16, *cmp_weights)
    tq_c = min(NSA_CQ, S)
    tq = min(NSA_TQ, S)
    tk = min(NSA_TK, S)
    o_cmp, selb, flags = _nsa_cmp(main, kvc, consts["cover"], consts["tile_of_block"], tq_c, slopes)
    flags = (flags[:, 0, :S // tk] > 0.0).astype(jnp.int32).reshape(-1)
    rhs = jnp.concatenate([main[:, COL_KVSEL:COL_KVSEL + HEAD_DIM].T, consts["key_features"]], axis=0)
    bgate = jnp.pad(b_gate, (GATE_COL_NSA, N_GATE - GATE_COL_NSA - b_gate.shape[0])).reshape(1, N_GATE)
    return _nsa_attend(flags, main, rhs, selb, o_cmp, gates, bgate, tq, tk, slopes)


def _dilated_mixer(main, xd4, xd16, slopes):
    S = main.shape[0]
    main_cols = (COL_QB // PAIR, COL_KB // PAIR, COL_VB // PAIR)
    split_cols = (0, GROUP_WIDTH // PAIR, 2 * GROUP_WIDTH // PAIR)
    outs, lses = [], []
    for (window, d), src, cols in zip(DILATED_PATTERNS, (main[None], xd4, xd16), (main_cols, split_cols, split_cols)):
        assert src.shape[0] == d
        o, lse = _band(src, cols, d, window // d, slopes, min(BAND_TQ, S // d // 2))
        outs.append(o)
        lses.append(lse)
    return _dil_combine(outs, lses, min(512, S))


def _fox_mixer(main, gates, b_f):
    S = main.shape[0]
    H = HEADS_PER_MIXER
    f = gates[:, GATE_COL_F:GATE_COL_F + H].T
    f = jnp.concatenate([f, jnp.zeros_like(f)], axis=0)
    b = jnp.concatenate([b_f, jnp.zeros_like(b_f)]).reshape(2 * H, 1)
    c = _logf_cumsum(f, b, min(512, S))[:H].reshape(2, 2, S)
    return _fox(main, c, min(FOX_TQ, S), min(FOX_TK, S))


def kernel(x, p, g_mix, w_in, b_nsa_gate, b_forget, cmp_pos_k, cmp_w1_k, cmp_w2_k, cmp_pos_v, cmp_w1_v, cmp_w2_v,
           g_head, w_out, g_mlp, w_up, w_down, g_ple, w_ple_gate, b_ple_gate, w_ple_proj, g_final):
    B, S, D = x.shape
    depth = w_in.shape[0]
    slopes = _alibi_slopes()
    slopes_nsa = tuple(float(s) for s in slopes[0::2])
    slopes_dil = tuple(float(s) for s in slopes[1::2])
    perm, colscale = _inproj_columns()
    n_pad = N_MAIN + N_GATE - perm.shape[0]

    n_blk = S // NSA_SEL_LEN
    n_cmp_pad = S // NSA_CMP_STRIDE
    cstart = np.arange(n_cmp_pad) * NSA_CMP_STRIDE
    bstart = np.arange(n_blk) * NSA_SEL_LEN
    cover = (cstart[:, None] < bstart[None, :] + NSA_SEL_LEN) & (cstart[:, None] + NSA_CMP_LEN - 1 >= bstart[None, :])
    tile_of_block = bstart[:, None] // min(NSA_TK, S) == np.arange(128)[None, :]
    pos = np.arange(S)
    key_features = np.zeros((HEAD_DIM + n_blk, S), np.float32)
    key_features[0] = pos // 128
    key_features[1] = pos % 128
    key_features[HEAD_DIM + pos // NSA_SEL_LEN, pos] = 1.0
    consts = {"cover": jnp.asarray(cover, BF16), "tile_of_block": jnp.asarray(tile_of_block, BF16),
              "key_features": jnp.asarray(key_features, BF16)}

    outs = []
    for b in range(B):
        h = x[b]
        for i in range(depth):
            w = jnp.pad(w_in[i][:, perm] * colscale, ((0, 0), (0, n_pad))).astype(BF16)
            main, gates, x16, xd4, xd16 = _inproj(h, g_mix[i].reshape(1, D), w, min(512, S))
            cmp_weights = _compress_weights(cmp_pos_k[i], cmp_w1_k[i], cmp_w2_k[i],
                                            cmp_pos_v[i], cmp_w1_v[i], cmp_w2_v[i])
            mixers = (_nsa_mixer(main, x16, gates, b_nsa_gate[i], cmp_weights, slopes_nsa, consts),
                      _dilated_mixer(main, xd4, xd16, slopes_dil),
                      _fox_mixer(main, gates, b_forget[i]),
                      _sb(main, min(SB_TQ, S), min(SB_TK, S)))
            h = _post(h, mixers, p[i, b], g_head[i].reshape(1, D), w_out[i].astype(BF16),
                      g_mlp[i].reshape(1, D), w_up[i].astype(BF16), w_down[i].astype(BF16),
                      g_ple[i].reshape(1, D), w_ple_gate[i].astype(BF16), b_ple_gate[i].reshape(1, D),
                      w_ple_proj[i].astype(BF16), g_final.reshape(1, D), min(256, S), i == depth - 1)
        outs.append(h)
    return jnp.stack(outs)
```

```python
import functools
import math

import numpy as np
import jax
import jax.numpy as jnp
from jax import lax
from jax.experimental import pallas as pl
from jax.experimental.pallas import tpu as pltpu

F32 = jnp.float32
BF16 = jnp.bfloat16

HEAD_DIM = 64
HEADS_PER_MIXER = 4
GROUP_WIDTH = HEADS_PER_MIXER * HEAD_DIM
PAIR = 2 * HEAD_DIM
N_ALIBI_HEADS = 2 * HEADS_PER_MIXER
NSA_CMP_LEN = 32
NSA_CMP_STRIDE = 16
NSA_CMP_HIDDEN = 256
NSA_SEL_LEN = 64
NSA_N_SEL = 16
NSA_WINDOW = 512
DILATED_PATTERNS = ((128, 1), (512, 4), (2048, 16))
RMS_EPS = 1e-6
SEL_FORCE = 1e9
QK_SCALE = HEAD_DIM ** -0.5

N_MAIN = 2944
N_GATE = 128
COL_QA = 0
COL_KVCMP, COL_KVSEL, COL_KVWIN = 256, 384, 512
COL_QB, COL_KB, COL_VB = 640, 896, 1152
COL_QC, COL_KC, COL_VC = 1408, 1664, 1920
COL_QD, COL_KD, COL_VD = 2176, 2432, 2688
GATE_COL_NSA, GATE_COL_F = 0, 12

NSA_CQ = 256
NSA_TQ, NSA_TK = 128, 512
NSA_CHAINS = 2
FOX_TQ, FOX_TK = 512, 512
SB_TQ, SB_TK = 512, 256
BAND_TQ = 512
EXP_ZERO = -104.0

MASKED = -1e9
M_INIT = -1e8
VMEM_LIMIT = 56 * 1024 * 1024


def _alibi_slopes():
    return 2.0 ** (-8.0 * np.arange(1, N_ALIBI_HEADS + 1) / N_ALIBI_HEADS)


def _params(*sem):
    return pltpu.CompilerParams(dimension_semantics=sem, vmem_limit_bytes=VMEM_LIMIT)


def _resident(shape, index_map):
    return pl.BlockSpec(shape, index_map, pipeline_mode=pl.Buffered(1))


def _dot(a, b):
    return jnp.dot(a, b, preferred_element_type=F32)


def _dot_nt(a, b):
    return lax.dot_general(a, b, (((1,), (1,)), ((), ())), preferred_element_type=F32)


def _split3(x):
    hi = x.astype(BF16)
    r = x - hi.astype(F32)
    mid = r.astype(BF16)
    lo = (r - mid.astype(F32)).astype(BF16)
    return hi, mid, lo


def _dot3(x, w01):
    hi, mid, lo = _split3(x)
    return _dot(hi, w01) + _dot(mid, w01) + _dot(lo, w01)


def _rms(x, g):
    return x * lax.rsqrt(jnp.mean(x * x, axis=-1, keepdims=True) + RMS_EPS) * g


def _softplus(z):
    return jnp.maximum(z, 0.0) + jnp.log(1.0 + jnp.exp(-jnp.abs(z)))


def _head_lanes(shape, hh):
    lane = lax.broadcasted_iota(jnp.int32, shape, len(shape) - 1)
    return lane < HEAD_DIM if hh == 0 else lane >= HEAD_DIM


def _inproj_kernel(h_ref, g_ref, w_ref, main_ref, gate_ref, x16_ref, xd4_ref, xd16_ref, rows_ref):
    tm = h_ref.shape[0]
    u = _rms(h_ref[...], g_ref[...]).astype(BF16)
    acc = _dot(u, w_ref[...])
    main_ref[...] = acc[:, :N_MAIN].astype(main_ref.dtype)
    gate_ref[...] = acc[:, N_MAIN:]
    n_pairs = 3 * GROUP_WIDTH // PAIR
    rows_ref[0] = acc[:, COL_KVCMP:COL_KVCMP + PAIR]
    for c in range(n_pairs):
        rows_ref[1 + c] = acc[:, COL_QB + c * PAIR:COL_QB + (c + 1) * PAIR]
    n16 = tm // NSA_CMP_STRIDE
    for l in range(NSA_CMP_STRIDE):
        x16_ref[:, l * PAIR:(l + 1) * PAIR] = rows_ref[0, pl.ds(l, n16, stride=NSA_CMP_STRIDE), :].astype(BF16)
    for out_ref in (xd4_ref, xd16_ref):
        d = out_ref.shape[0]
        for r in range(d):
            for c in range(n_pairs):
                out_ref[r, :, c * PAIR:(c + 1) * PAIR] = rows_ref[1 + c, pl.ds(r, tm // d, stride=d), :].astype(BF16)


def _inproj(h, g, w, tm):
    S, D = h.shape
    n = w.shape[1]
    d4, d16 = DILATED_PATTERNS[1][1], DILATED_PATTERNS[2][1]
    wide = 3 * GROUP_WIDTH
    return pl.pallas_call(
        _inproj_kernel,
        grid=(S // tm,),
        in_specs=[pl.BlockSpec((tm, D), lambda i: (i, 0)),
                  _resident((1, D), lambda i: (0, 0)),
                  _resident((D, n), lambda i: (0, 0))],
        out_specs=[pl.BlockSpec((tm, N_MAIN), lambda i: (i, 0)),
                   pl.BlockSpec((tm, N_GATE), lambda i: (i, 0)),
                   pl.BlockSpec((tm // NSA_CMP_STRIDE, NSA_CMP_STRIDE * PAIR), lambda i: (i, 0)),
                   pl.BlockSpec((d4, tm // d4, wide), lambda i: (0, i, 0)),
                   pl.BlockSpec((d16, tm // d16, wide), lambda i: (0, i, 0))],
        out_shape=[jax.ShapeDtypeStruct((S, N_MAIN), BF16), jax.ShapeDtypeStruct((S, N_GATE), F32),
                   jax.ShapeDtypeStruct((S // NSA_CMP_STRIDE, NSA_CMP_STRIDE * PAIR), BF16),
                   jax.ShapeDtypeStruct((d4, S // d4, wide), BF16),
                   jax.ShapeDtypeStruct((d16, S // d16, wide), BF16)],
        scratch_shapes=[pltpu.VMEM((1 + wide // PAIR, tm, PAIR), F32)],
        compiler_params=_params("parallel"),
        name="inproj",
    )(h, g, w)


def _compress_kernel(x_ref, pos_ref, w1_ref, w2_ref, out_ref):
    x = x_ref[...].astype(F32)
    n = x.shape[0]
    top = _dot((x + pos_ref[0:1, :]).astype(BF16), w1_ref[0])
    bot = _dot((x + pos_ref[1:2, :]).astype(BF16), w1_ref[1])
    hid = top + pltpu.roll(bot, n - 1, 0)
    c = math.sqrt(2.0 / math.pi)
    hid = 0.5 * hid * (1.0 + jnp.tanh(c * (hid + 0.044715 * (hid * hid * hid))))
    out_ref[...] = _dot(hid.astype(BF16), w2_ref[...]).astype(out_ref.dtype)


def _compress(x16, pos, w1, w2):
    n = x16.shape[0]
    return pl.pallas_call(
        _compress_kernel,
        out_shape=jax.ShapeDtypeStruct((n, PAIR), BF16),
        compiler_params=pltpu.CompilerParams(vmem_limit_bytes=VMEM_LIMIT),
        name="nsa_compress",
    )(x16, pos, w1, w2)


def _nsa_cmp_kernel(q_ref, kvc_ref, cover_ref, tile_ref, oc_ref, selb_ref, flag_ref, *, tq, slopes):
    t0 = pl.program_id(0) * tq
    nc = kvc_ref.shape[0]
    nb = cover_ref.shape[1]
    tpos = t0 + lax.broadcasted_iota(jnp.int32, (tq, nc), 0)
    jidx = lax.broadcasted_iota(jnp.int32, (tq, nc), 1)
    mask = jidx * NSA_CMP_STRIDE + (NSA_CMP_LEN - 1) <= tpos
    dist = tpos.astype(F32) - (jidx.astype(F32) * NSA_CMP_STRIDE + (NSA_CMP_LEN - 1) / 2.0)
    kvc = kvc_ref[...]
    q = q_ref[...]
    zeros = jnp.zeros((tq, HEAD_DIM), BF16)
    psum = jnp.zeros((tq, nc), F32)
    outs = []
    for h in range(HEADS_PER_MIXER):
        qh = jnp.concatenate([q[:, h * HEAD_DIM:(h + 1) * HEAD_DIM], zeros], axis=1)
        s = _dot_nt(qh, kvc) - slopes[h] * dist
        m = jnp.max(jnp.where(mask, s, -jnp.inf), axis=-1, keepdims=True)
        m = jnp.where(m > -jnp.inf, m, 0.0)
        e = jnp.where(mask, jnp.exp(s - m), 0.0)
        den = jnp.sum(e, axis=-1, keepdims=True)
        p = e / jnp.maximum(den, 1e-30)
        outs.append(_dot(p.astype(BF16), kvc)[:, HEAD_DIM:])
        psum = psum + p
    oc_ref[...] = jnp.concatenate(outs, axis=1)
    imp = _dot3(psum, cover_ref[...])
    bidx = lax.broadcasted_iota(jnp.int32, (tq, nb), 1)
    qpos = t0 + lax.broadcasted_iota(jnp.int32, (tq, nb), 0)
    cur = qpos // NSA_SEL_LEN
    forced = jnp.where(bidx == 0, 1.0, jnp.where(bidx == cur, 1.0, jnp.where(bidx == cur - 1, 1.0, 0.0)))
    elig = bidx * NSA_SEL_LEN <= qpos
    n_forced = jnp.sum(forced, axis=-1, keepdims=True)
    score = jnp.where(forced > 0.0, -jnp.inf, jnp.where(elig, imp, -SEL_FORCE))
    bf = bidx.astype(F32)

    def pick(state, allow=None):
        sel, score = state
        mx = jnp.max(score, axis=-1, keepdims=True)
        first = jnp.min(jnp.where(score == mx, bf, float(nb)), axis=-1, keepdims=True)
        if allow is not None:
            first = jnp.where(allow, first, -1.0)
        hit = bf == first
        return jnp.where(hit, 1.0, sel), jnp.where(hit, -jnp.inf, score)

    state = (forced, score)
    for _ in range(NSA_N_SEL - 3):
        state = pick(state)

    def early(state):
        for short in (1, 2):
            state = pick(state, n_forced <= 3.0 - short)
        return state

    sel, _ = lax.cond(t0 < 2 * NSA_SEL_LEN, early, lambda st: st, state)
    sel = jnp.where(elig, sel, 0.0)
    selb_ref[...] = jnp.where(sel > 0.0, 0.0, MASKED).astype(selb_ref.dtype)
    for part in range(tq // NSA_TQ):
        any_q = jnp.max(sel[part * NSA_TQ:(part + 1) * NSA_TQ], axis=0, keepdims=True).astype(BF16)
        flag_ref[part] = _dot(jnp.broadcast_to(any_q, (8, nb)), tile_ref[...])


def _nsa_cmp(main, kvc, cover, tile_of_block, tq, slopes):
    S = main.shape[0]
    nc = kvc.shape[0]
    nb = cover.shape[1]
    parts = tq // NSA_TQ
    return pl.pallas_call(
        functools.partial(_nsa_cmp_kernel, tq=tq, slopes=slopes),
        grid=(S // tq,),
        in_specs=[pl.BlockSpec((tq, GROUP_WIDTH), lambda i: (i, COL_QA // GROUP_WIDTH)),
                  _resident((nc, PAIR), lambda i: (0, 0)),
                  _resident((nc, nb), lambda i: (0, 0)),
                  _resident((nb, 128), lambda i: (0, 0))],
        out_specs=[pl.BlockSpec((tq, GROUP_WIDTH), lambda i: (i, 0)),
                   pl.BlockSpec((tq, nb), lambda i: (i, 0)),
                   pl.BlockSpec((parts, 8, 128), lambda i: (i, 0, 0))],
        out_shape=[jax.ShapeDtypeStruct((S, GROUP_WIDTH), F32), jax.ShapeDtypeStruct((S, nb), BF16),
                   jax.ShapeDtypeStruct((S // NSA_TQ, 8, 128), F32)],
        compiler_params=_params("parallel"),
        name="nsa_cmp_select",
    )(main, kvc, cover, tile_of_block)


def _nsa_attend_kernel(tiles_ref, count_ref, q_ref, rhs_ref, kvsel_ref, kvwin_ref, selb_ref, ocmp_ref, gate_ref,
                       bgate_ref, o_ref, lhs_ref, *, tq, tk, slopes):
    H = HEADS_PER_MIXER
    S = kvsel_ref.shape[0]
    i = pl.program_id(0)
    t0 = i * tq
    list_base = i * (S // tk)
    rows = H * tq
    nb = selb_ref.shape[1]

    q = q_ref[...]
    selb = selb_ref[...]
    lane = lax.broadcasted_iota(jnp.int32, (tq, HEAD_DIM), 1)
    for h in range(H):
        aug = jnp.where(lane == 0, 128.0 * slopes[h], jnp.where(lane == 1, slopes[h], 0.0)).astype(BF16)
        lhs_ref[h * tq:(h + 1) * tq, 0:PAIR] = jnp.concatenate([q[:, h * HEAD_DIM:(h + 1) * HEAD_DIM], aug], axis=1)
        lhs_ref[h * tq:(h + 1) * tq, PAIR:PAIR + nb] = selb
    group = rows // NSA_CHAINS

    def tile(j, carries, diagonal):
        ks = pl.multiple_of(j * tk, tk)
        rhs = rhs_ref[:, pl.ds(ks, tk)]
        kv = kvsel_ref[pl.ds(ks, tk), :]
        new = []
        for g in range(NSA_CHAINS):
            m_old, l, acc = carries[g]
            s = _dot(lhs_ref[g * group:(g + 1) * group, :], rhs)
            if diagonal:
                qpos = jnp.concatenate([t0 + lax.broadcasted_iota(jnp.int32, (tq, tk), 0)] * (group // tq), axis=0)
                kpos = ks + lax.broadcasted_iota(jnp.int32, (group, tk), 1)
                s = jnp.where(kpos <= qpos, s, MASKED)
            m_new = jnp.maximum(m_old, jnp.max(s, axis=-1, keepdims=True))
            p = jnp.exp(s - m_new)
            alpha = jnp.exp(m_old - m_new)
            l = alpha * l + jnp.sum(p, axis=-1, keepdims=True)
            acc = alpha * acc + _dot(p.astype(BF16), kv)
            new.append((m_new, l, acc))
        return tuple(new)

    init = (jnp.full((group, 1), M_INIT, F32), jnp.zeros((group, 1), F32), jnp.zeros((group, PAIR), F32))
    carries = lax.fori_loop(0, count_ref[i], lambda n, c: tile(tiles_ref[list_base + n], c, False),
                            (init,) * NSA_CHAINS)
    carries = tile((t0 + tq - 1) // tk, carries, True)
    l_all = jnp.concatenate([c[1] for c in carries], axis=0)
    acc_all = jnp.concatenate([c[2] for c in carries], axis=0)
    o_sel = acc_all[:, HEAD_DIM:] / jnp.maximum(l_all, 1e-30)

    back = NSA_WINDOW - 1
    width = NSA_WINDOW + tq
    start = pl.multiple_of(jnp.maximum(t0 - NSA_WINDOW, 0), tq)
    kw = kvwin_ref[pl.ds(start, width), :]
    qz = jnp.where(_head_lanes((rows, PAIR), 0), lhs_ref[:, 0:PAIR], jnp.zeros((rows, PAIR), BF16))
    sw = _dot_nt(qz, kw)
    qpos = jnp.concatenate([t0 + lax.broadcasted_iota(jnp.int32, (tq, width), 0)] * H, axis=0)
    dist = qpos - (start + lax.broadcasted_iota(jnp.int32, (rows, width), 1))
    slope = jnp.concatenate([jnp.full((tq, 1), slopes[h], F32) for h in range(H)], axis=0)
    sw = sw - slope * dist.astype(F32)
    sw = jnp.where(dist >= 0, jnp.where(dist <= back, sw, -jnp.inf), -jnp.inf)
    mw = jnp.max(sw, axis=-1, keepdims=True)
    ew = jnp.exp(sw - mw)
    o_win = _dot(ew.astype(BF16), kw)[:, HEAD_DIM:] / jnp.sum(ew, axis=-1, keepdims=True)

    g = jax.nn.sigmoid(gate_ref[...] + bgate_ref[...])
    ocmp = ocmp_ref[...]
    outs = []
    for h in range(H):
        r0 = h * tq
        c0 = GATE_COL_NSA + h
        outs.append(g[:, c0:c0 + 1] * ocmp[:, h * HEAD_DIM:(h + 1) * HEAD_DIM]
                    + g[:, c0 + H:c0 + H + 1] * o_sel[r0:r0 + tq]
                    + g[:, c0 + 2 * H:c0 + 2 * H + 1] * o_win[r0:r0 + tq])
    o_ref[...] = _head_norm(jnp.concatenate(outs, axis=1))


def _nsa_attend(tiles, counts, main, rhs, selb, ocmp, gates, bgate, tq, tk, slopes):
    S = main.shape[0]
    nb = selb.shape[1]
    rows = HEADS_PER_MIXER * tq
    grid_spec = pltpu.PrefetchScalarGridSpec(
        num_scalar_prefetch=2,
        grid=(S // tq,),
        in_specs=[pl.BlockSpec((tq, GROUP_WIDTH), lambda i, *_: (i, COL_QA // GROUP_WIDTH)),
                  _resident((rhs.shape[0], S), lambda i, *_: (0, 0)),
                  _resident((S, PAIR), lambda i, *_: (0, COL_KVSEL // PAIR)),
                  _resident((S, PAIR), lambda i, *_: (0, COL_KVWIN // PAIR)),
                  pl.BlockSpec((tq, nb), lambda i, *_: (i, 0)),
                  pl.BlockSpec((tq, GROUP_WIDTH), lambda i, *_: (i, 0)),
                  pl.BlockSpec((tq, N_GATE), lambda i, *_: (i, 0)),
                  _resident((1, N_GATE), lambda i, *_: (0, 0))],
        out_specs=pl.BlockSpec((tq, GROUP_WIDTH), lambda i, *_: (i, 0)),
        scratch_shapes=[pltpu.VMEM((rows, PAIR + nb), BF16)],
    )
    return pl.pallas_call(
        functools.partial(_nsa_attend_kernel, tq=tq, tk=tk, slopes=slopes),
        grid_spec=grid_spec,
        out_shape=jax.ShapeDtypeStruct((S, GROUP_WIDTH), F32),
        compiler_params=_params("parallel"),
        name="nsa_attend",
    )(tiles, counts, main, rhs, main, main, selb, ocmp, gates, bgate)


def _band_kernel(q_ref, k_ref, v_ref, dist_ref, o_ref, lse_ref, *, tq, back_pad, slopes, d):
    pr = pl.program_id(1)
    q0 = pl.multiple_of(pl.program_id(2) * tq, tq)
    width = back_pad + tq
    start = pl.multiple_of(jnp.maximum(q0 - back_pad, 0), 128)
    k = k_ref[0, pl.ds(start, width), :]
    v = v_ref[0, pl.ds(start, width), :]
    q = q_ref[0]
    dist = dist_ref[jnp.where(q0 < back_pad, 0, 1)]
    o_pair, lse_pair = [], []
    for hh in range(2):
        slope = jnp.where(pr == 0, slopes[hh], slopes[2 + hh]) * d
        qh = jnp.where(_head_lanes(q.shape, hh), q, jnp.zeros_like(q))
        s = _dot_nt(qh, k) - slope * dist
        m = jnp.max(s, axis=-1, keepdims=True)
        e = jnp.exp(s - m)
        l = jnp.sum(e, axis=-1, keepdims=True)
        o_pair.append(_dot(e.astype(BF16), v) / l)
        lse_pair.append(m + jnp.log(l))
    first = _head_lanes((tq, PAIR), 0)
    o_ref[0] = jnp.where(first, o_pair[0], o_pair[1])
    lse_ref[0] = jnp.where(first, lse_pair[0], lse_pair[1])


def _band(src, cols, d, back, slopes, tq):
    _, L, _ = src.shape
    back_pad = -(-back // 128) * 128
    width = back_pad + tq
    r_ = np.arange(tq)[:, None]
    c_ = np.arange(width)[None, :]
    dist = np.stack([r_ - c_, r_ + back_pad - c_]).astype(np.float32)
    dist = np.where((dist >= 0) & (dist <= back), dist, 1e30)
    cq, ck, cv = cols
    out_spec = pl.BlockSpec((1, tq, PAIR), lambda r, p_, i: (r, i, p_))
    return pl.pallas_call(
        functools.partial(_band_kernel, tq=tq, back_pad=back_pad, slopes=slopes, d=float(d)),
        grid=(d, 2, L // tq),
        in_specs=[pl.BlockSpec((1, tq, PAIR), lambda r, p_, i: (r, i, cq + p_)),
                  pl.BlockSpec((1, L, PAIR), lambda r, p_, i: (r, 0, ck + p_)),
                  pl.BlockSpec((1, L, PAIR), lambda r, p_, i: (r, 0, cv + p_)),
                  _resident((2, tq, width), lambda r, p_, i: (0, 0, 0))],
        out_specs=[out_spec, out_spec],
        out_shape=[jax.ShapeDtypeStruct((d, L, GROUP_WIDTH), F32)] * 2,
        compiler_params=_params("parallel", "parallel", "parallel"),
        name="band_attention",
    )(src, src, src, jnp.asarray(dist, F32))


def _head_norm(o):
    lane = lax.broadcasted_iota(jnp.int32, o.shape, 1)
    o2 = o * o
    scale = jnp.zeros_like(o)
    for h in range(o.shape[1] // HEAD_DIM):
        mine = jnp.logical_and(lane >= h * HEAD_DIM, lane < (h + 1) * HEAD_DIM)
        ms = jnp.sum(jnp.where(mine, o2, 0.0), axis=-1, keepdims=True) * (1.0 / HEAD_DIM)
        scale = jnp.where(mine, lax.rsqrt(ms + RMS_EPS), scale)
    return o * scale


def _dil_combine_kernel(o1_ref, l1_ref, o4_ref, l4_ref, o16_ref, l16_ref, out_ref, o4_s, l4_s, o16_s, l16_s):
    tm, width = out_ref.shape
    n_pairs = width // PAIR
    for src, dst in ((o4_ref, o4_s), (l4_ref, l4_s), (o16_ref, o16_s), (l16_ref, l16_s)):
        d = src.shape[0]
        for r in range(d):
            for c in range(n_pairs):
                dst[c, pl.ds(r, tm // d, stride=d), :] = src[r, :, c * PAIR:(c + 1) * PAIR]
    token_order = lambda s: jnp.concatenate([s[c] for c in range(n_pairs)], axis=1)
    l1, l2, l3 = l1_ref[0], token_order(l4_s), token_order(l16_s)
    m = jnp.maximum(l1, jnp.maximum(l2, l3))
    e1, e2, e3 = jnp.exp(l1 - m), jnp.exp(l2 - m), jnp.exp(l3 - m)
    den = e1 + e2 + e3
    out_ref[...] = _head_norm((e1 / den) * o1_ref[0] + (e2 / den) * token_order(o4_s)
                              + (e3 / den) * token_order(o16_s))


def _dil_combine(outs, lses, tm):
    S, W = outs[0].shape[1:]
    specs = []
    for o in outs:
        d = o.shape[0]
        specs += [pl.BlockSpec((d, tm // d, W), lambda i: (0, i, 0))] * 2
    args = [a for pair in zip(outs, lses) for a in pair]
    return pl.pallas_call(
        _dil_combine_kernel,
        grid=(S // tm,),
        in_specs=specs,
        out_specs=pl.BlockSpec((tm, W), lambda i: (i, 0)),
        out_shape=jax.ShapeDtypeStruct((S, W), F32),
        scratch_shapes=[pltpu.VMEM((W // PAIR, tm, PAIR), F32)] * 4,
        compiler_params=_params("parallel"),
        name="dilated_combine",
    )(*args)


def _logf_cumsum_kernel(f_ref, b_ref, tri_ref, c_ref, carry_ref):
    @pl.when(pl.program_id(0) == 0)
    def _():
        carry_ref[...] = jnp.zeros_like(carry_ref)

    x = f_ref[...] + b_ref[...]
    logf = jnp.minimum(x, 0.0) - jnp.log1p(jnp.exp(-jnp.abs(x)))
    c = _dot3(logf, tri_ref[...]) + carry_ref[:, :1]
    c_ref[...] = c
    carry_ref[...] = jnp.broadcast_to(c[:, -1:], carry_ref.shape)


def _logf_cumsum(f, b, tk):
    R, S = f.shape
    tri = (np.arange(tk)[:, None] <= np.arange(tk)[None, :]).astype(np.float32)
    return pl.pallas_call(
        _logf_cumsum_kernel,
        grid=(S // tk,),
        in_specs=[pl.BlockSpec((R, tk), lambda i: (0, i)),
                  pl.BlockSpec((R, 1), lambda i: (0, 0)),
                  _resident((tk, tk), lambda i: (0, 0))],
        out_specs=pl.BlockSpec((R, tk), lambda i: (0, i)),
        out_shape=jax.ShapeDtypeStruct((R, S), F32),
        scratch_shapes=[pltpu.VMEM((R, 128), F32)],
        compiler_params=_params("arbitrary"),
        name="logf_cumsum",
    )(f, b, jnp.asarray(tri, BF16))


def _pair_key_norms(k_ref, kmax_ref, chunk):
    S = k_ref.shape[0]
    first = _head_lanes((chunk, PAIR), 0)

    def body(ci, best):
        k = k_ref[pl.ds(pl.multiple_of(ci * chunk, chunk), chunk), :].astype(F32)
        sq = k * k
        n0 = jnp.max(jnp.sum(jnp.where(first, sq, 0.0), axis=-1, keepdims=True), axis=0, keepdims=True)
        n1 = jnp.max(jnp.sum(jnp.where(first, 0.0, sq), axis=-1, keepdims=True), axis=0, keepdims=True)
        return jnp.maximum(best[0], n0), jnp.maximum(best[1], n1)

    zero = jnp.zeros((1, 1), F32)
    n0, n1 = lax.fori_loop(0, S // chunk, body, (zero, zero))
    kmax_ref[0:1, :] = jnp.broadcast_to(jnp.sqrt(n0), (1, kmax_ref.shape[1]))
    kmax_ref[1:2, :] = jnp.broadcast_to(jnp.sqrt(n1), (1, kmax_ref.shape[1]))


def _fox_kernel(q_ref, k_ref, v_ref, c_ref, o_ref, kmax_ref, *, tq, tk):
    i = pl.program_id(1)
    first = (i * tq) // tk

    @pl.when(i == 0)
    def _():
        _pair_key_norms(k_ref, kmax_ref, min(1024, k_ref.shape[0]))

    q = q_ref[...]
    row = lax.broadcasted_iota(jnp.int32, (tq, tk), 0) + (i * tq - first * tk)
    col = lax.broadcasted_iota(jnp.int32, (tq, tk), 1)
    qs, reach = [], []
    for hh in range(2):
        qh = jnp.where(_head_lanes(q.shape, hh), q, jnp.zeros_like(q))
        qf = qh.astype(F32)
        qs.append(qh)
        reach.append(jnp.sqrt(jnp.sum(qf * qf, axis=-1, keepdims=True)) * kmax_ref[hh:hh + 1, :1])

    def tile(j, carries, diagonal):
        ks = pl.multiple_of(j * tk, tk)
        k = k_ref[pl.ds(ks, tk), :]
        v = v_ref[pl.ds(ks, tk), :]
        new, gap = [], None
        for hh in range(2):
            m, l, acc = carries[hh]
            c_tile = c_ref[0, hh:hh + 1, pl.ds(ks, tk)]
            s = _dot_nt(qs[hh], k) - c_tile
            if diagonal:
                s = jnp.where(col <= row, s, -jnp.inf)
            m_new = jnp.maximum(m, jnp.max(s, axis=-1, keepdims=True))
            p = jnp.exp(s - m_new)
            alpha = jnp.exp(m - m_new)
            l = alpha * l + jnp.sum(p, axis=-1, keepdims=True)
            acc = alpha * acc + _dot(p.astype(BF16), v)
            new.append((m_new, l, acc))
            g = jnp.max(reach[hh] - c_tile[:, :1] - m_new)
            gap = g if gap is None else jnp.maximum(gap, g)
        return tuple(new), gap

    init = (jnp.full((tq, 1), -jnp.inf, F32), jnp.zeros((tq, 1), F32), jnp.zeros((tq, PAIR), F32))
    carries, gap = tile(first, (init, init), True)

    def more(state):
        j, gap, _ = state
        return jnp.logical_and(j >= 0, gap >= EXP_ZERO - 1.0)

    def body(state):
        j, _, carries = state
        carries, gap = tile(j, carries, False)
        return j - 1, gap, carries

    _, _, carries = lax.while_loop(more, body, (first - 1, gap, carries))
    outs = [acc / l for _, l, acc in carries]
    o_ref[...] = _head_norm(jnp.where(_head_lanes((tq, PAIR), 0), outs[0], outs[1]))


def _pair_specs(S, tq, col_q, col_k, col_v):
    return [pl.BlockSpec((tq, PAIR), lambda p_, i: (i, col_q // PAIR + p_)),
            pl.BlockSpec((S, PAIR), lambda p_, i: (0, col_k // PAIR + p_)),
            pl.BlockSpec((S, PAIR), lambda p_, i: (0, col_v // PAIR + p_))]


def _fox(main, c, tq, tk):
    S = main.shape[0]
    return pl.pallas_call(
        functools.partial(_fox_kernel, tq=tq, tk=tk),
        grid=(2, S // tq),
        in_specs=_pair_specs(S, tq, COL_QC, COL_KC, COL_VC) + [pl.BlockSpec((1, 2, S), lambda p_, i: (p_, 0, 0))],
        out_specs=pl.BlockSpec((tq, PAIR), lambda p_, i: (i, p_)),
        out_shape=jax.ShapeDtypeStruct((S, GROUP_WIDTH), F32),
        scratch_shapes=[pltpu.VMEM((8, 128), F32)],
        compiler_params=_params("parallel", "arbitrary"),
        name="forgetting_attention",
    )(main, main, main, c)


def _sb_kernel(q_ref, k_ref, v_ref, tri_ref, o_ref, *, tq, tk):
    i = pl.program_id(1)
    per_q = tq // tk
    q = q_ref[...]
    tri = tri_ref[...]
    row = lax.broadcasted_iota(jnp.int32, (tq, tk), 0)
    col = lax.broadcasted_iota(jnp.int32, (tq, tk), 1)
    qs = [jnp.where(_head_lanes(q.shape, hh), q, jnp.zeros_like(q)) for hh in range(2)]

    def tile(j, carries, lead):
        ks = pl.multiple_of(j * tk, tk)
        k = k_ref[pl.ds(ks, tk), :]
        v = v_ref[pl.ds(ks, tk), :]
        strict = None if lead is None else col + lead * tk < row
        new, top = [], None
        for hh in range(2):
            run, acc = carries[hh]
            z = _dot_nt(qs[hh], k)
            log_keep = -_softplus(z)
            log_beta = z + log_keep
            if strict is not None:
                log_keep = jnp.where(strict, log_keep, 0.0)
            later = run + _dot3(log_keep, tri)
            a = jnp.exp(log_beta + later)
            if strict is not None:
                a = jnp.where(strict, a, 0.0)
            acc = acc + _dot(a.astype(BF16), v)
            run = run + jnp.sum(log_keep, axis=-1, keepdims=True)
            new.append((run, acc))
            t = jnp.max(run)
            top = t if top is None else jnp.maximum(top, t)
        return tuple(new), top

    init = (jnp.zeros((tq, 1), F32), jnp.zeros((tq, PAIR), F32))
    carries, top = (init, init), None
    for lead in reversed(range(per_q)):
        carries, top = tile(i * per_q + lead, carries, lead)

    def more(state):
        j, top, _ = state
        return jnp.logical_and(j >= 0, top >= EXP_ZERO)

    def body(state):
        j, _, carries = state
        carries, top = tile(j, carries, None)
        return j - 1, top, carries

    _, _, carries = lax.while_loop(more, body, (i * per_q - 1, top, carries))
    o_ref[...] = _head_norm(jnp.where(_head_lanes((tq, PAIR), 0), carries[0][1], carries[1][1]))


def _sb(main, tq, tk):
    S = main.shape[0]
    tri = (np.arange(tk)[:, None] > np.arange(tk)[None, :]).astype(np.float32)
    return pl.pallas_call(
        functools.partial(_sb_kernel, tq=tq, tk=tk),
        grid=(2, S // tq),
        in_specs=_pair_specs(S, tq, COL_QD, COL_KD, COL_VD) + [_resident((tk, tk), lambda p_, i: (0, 0))],
        out_specs=pl.BlockSpec((tq, PAIR), lambda p_, i: (i, p_)),
        out_shape=jax.ShapeDtypeStruct((S, GROUP_WIDTH), F32),
        compiler_params=_params("parallel", "parallel"),
        name="stick_breaking_attention",
    )(main, main, main, jnp.asarray(tri, BF16))


def _post_kernel(h_ref, oa_ref, ob_ref, oc_ref, od_ref, p_ref, ghead_ref, wout_ref, gmlp_ref, wup_ref,
                 wdown_ref, gple_ref, wgate_ref, bgate_ref, wproj_ref, gfinal_ref, out_ref, *, f_chunk, final):
    o = jnp.concatenate([oa_ref[...], ob_ref[...], oc_ref[...], od_ref[...]], axis=1)
    on = o * ghead_ref[...]
    h = h_ref[...] + _dot(on.astype(BF16), wout_ref[...])
    u = _rms(h, gmlp_ref[...]).astype(BF16)
    d_ff = wup_ref.shape[1]
    acc = jnp.zeros(h.shape, F32)
    for f in range(0, d_ff, f_chunk):
        hid = jnp.maximum(_dot(u, wup_ref[:, f:f + f_chunk]), 0.0)
        acc = acc + _dot((hid * hid).astype(BF16), wdown_ref[f:f + f_chunk, :])
    h = h + acc
    gate = jax.nn.sigmoid(_dot(_rms(h, gple_ref[...]).astype(BF16), wgate_ref[...]) + bgate_ref[...])
    h = h + _dot(p_ref[...].astype(BF16), wproj_ref[...]) * gate
    out_ref[...] = _rms(h, gfinal_ref[...]) if final else h


def _post(h, mixers, p, ghead, wout, gmlp, wup, wdown, gple, wgate, bgate, wproj, gfinal, tm, final):
    S, D = h.shape
    row = lambda i: (i, 0)
    fixed = lambda i: (0, 0)
    vec = _resident((1, D), fixed)
    return pl.pallas_call(
        functools.partial(_post_kernel, f_chunk=1024, final=final),
        grid=(S // tm,),
        in_specs=[pl.BlockSpec((tm, D), row)] + [pl.BlockSpec((tm, GROUP_WIDTH), row)] * 4
                 + [pl.BlockSpec((tm, p.shape[1]), row),
                    vec, _resident(wout.shape, fixed),
                    vec, _resident(wup.shape, fixed), _resident(wdown.shape, fixed),
                    vec, _resident(wgate.shape, fixed), vec, _resident(wproj.shape, fixed), vec],
        out_specs=pl.BlockSpec((tm, D), row),
        out_shape=jax.ShapeDtypeStruct((S, D), F32),
        compiler_params=_params("parallel"),
        name="outproj_mlp_ple",
    )(h, *mixers, p, ghead, wout, gmlp, wup, wdown, gple, wgate, bgate, wproj, gfinal)


def _inproj_columns():
    splits = (256, 64, 64, 64, 64, 64, 64, 12, 256, 256, 256, 256, 256, 256, 4, 256, 256, 256)
    offs = np.concatenate([[0], np.cumsum(splits)])
    seg = lambda k: np.arange(offs[k], offs[k + 1])
    main = [0, 1, 2, 3, 4, 5, 6, 8, 9, 10, 11, 12, 13, 15, 16, 17]
    perm = np.concatenate([seg(k) for k in main] + [seg(7), seg(14)])
    scale = np.ones(perm.shape[0], np.float32)
    for start in (COL_QA, COL_QB, COL_QC, COL_QD):
        scale[start:start + GROUP_WIDTH] = QK_SCALE
    return perm, scale


def _compress_weights(pos_k, w1_k, w2_k, pos_v, w1_v, w2_v):
    half = NSA_CMP_LEN // 2

    def expand_rows(wk, wv):
        z = jnp.zeros_like(wk).reshape(half, HEAD_DIM, -1)
        k_rows = jnp.concatenate([wk.reshape(half, HEAD_DIM, -1), z], axis=2)
        v_rows = jnp.concatenate([z, wv.reshape(half, HEAD_DIM, -1)], axis=2)
        return jnp.concatenate([k_rows, v_rows], axis=1).reshape(half * PAIR, -1)

    n_top = half * HEAD_DIM
    w1 = jnp.stack([expand_rows(w1_k[:n_top], w1_v[:n_top]), expand_rows(w1_k[n_top:], w1_v[n_top:])]).astype(BF16)
    pos = jnp.concatenate([pos_k, pos_v], axis=1)
    pos = jnp.stack([pos[:half].reshape(-1), pos[half:].reshape(-1)])
    zk = jnp.zeros_like(w2_k)
    w2 = jnp.concatenate([jnp.concatenate([w2_k, zk], axis=1), jnp.concatenate([zk, w2_v], axis=1)], axis=0)
    return pos, w1, w2.astype(BF16)


def _nsa_mixer(main, x16, gates, b_gate, cmp_weights, slopes, consts):
    S = main.shape[0]
    kvc = _compress(x16, *cmp_weights)
    tq_c = min(NSA_CQ, S)
    tq = min(NSA_TQ, S)
    tk = min(NSA_TK, S)
    o_cmp, selb, flags = _nsa_cmp(main, kvc, consts["cover"], consts["tile_of_block"], tq_c, slopes)
    n_kt = S // tk
    diag_tile = (np.arange(S // tq) * tq + tq - 1) // tk
    wanted = jnp.logical_and(flags[:, 0, :n_kt] > 0.0, np.arange(n_kt)[None, :] < diag_tile[:, None])
    tiles = jnp.argsort(jnp.logical_not(wanted), axis=1, stable=True).astype(jnp.int32).reshape(-1)
    counts = jnp.sum(wanted, axis=1, dtype=jnp.int32)
    rhs = jnp.concatenate([main[:, COL_KVSEL:COL_KVSEL + HEAD_DIM].T, consts["key_features"]], axis=0)
    bgate = jnp.pad(b_gate, (GATE_COL_NSA, N_GATE - GATE_COL_NSA - b_gate.shape[0])).reshape(1, N_GATE)
    return _nsa_attend(tiles, counts, main, rhs, selb, o_cmp, gates, bgate, tq, tk, slopes)


def _dilated_mixer(main, xd4, xd16, slopes):
    S = main.shape[0]
    main_cols = (COL_QB // PAIR, COL_KB // PAIR, COL_VB // PAIR)
    split_cols = (0, GROUP_WIDTH // PAIR, 2 * GROUP_WIDTH // PAIR)
    outs, lses = [], []
    for (window, d), src, cols in zip(DILATED_PATTERNS, (main[None], xd4, xd16), (main_cols, split_cols, split_cols)):
        assert src.shape[0] == d
        o, lse = _band(src, cols, d, window // d, slopes, min(BAND_TQ, S // d // 2))
        outs.append(o)
        lses.append(lse)
    return _dil_combine(outs, lses, min(512, S))


def _fox_mixer(main, gates, b_f):
    S = main.shape[0]
    H = HEADS_PER_MIXER
    f = gates[:, GATE_COL_F:GATE_COL_F + H].T
    f = jnp.concatenate([f, jnp.zeros_like(f)], axis=0)
    b = jnp.concatenate([b_f, jnp.zeros_like(b_f)]).reshape(2 * H, 1)
    c = _logf_cumsum(f, b, min(512, S))[:H].reshape(2, 2, S)
    return _fox(main, c, min(FOX_TQ, S), min(FOX_TK, S))


def kernel(x, p, g_mix, w_in, b_nsa_gate, b_forget, cmp_pos_k, cmp_w1_k, cmp_w2_k, cmp_pos_v, cmp_w1_v, cmp_w2_v,
           g_head, w_out, g_mlp, w_up, w_down, g_ple, w_ple_gate, b_ple_gate, w_ple_proj, g_final):
    B, S, D = x.shape
    depth = w_in.shape[0]
    slopes = _alibi_slopes()
    slopes_nsa = tuple(float(s) for s in slopes[0::2])
    slopes_dil = tuple(float(s) for s in slopes[1::2])
    perm, colscale = _inproj_columns()
    n_pad = N_MAIN + N_GATE - perm.shape[0]

    n_blk = S // NSA_SEL_LEN
    n_cmp_pad = S // NSA_CMP_STRIDE
    cstart = np.arange(n_cmp_pad) * NSA_CMP_STRIDE
    bstart = np.arange(n_blk) * NSA_SEL_LEN
    cover = (cstart[:, None] < bstart[None, :] + NSA_SEL_LEN) & (cstart[:, None] + NSA_CMP_LEN - 1 >= bstart[None, :])
    tile_of_block = bstart[:, None] // min(NSA_TK, S) == np.arange(128)[None, :]
    pos = np.arange(S)
    key_features = np.zeros((HEAD_DIM + n_blk, S), np.float32)
    key_features[0] = pos // 128
    key_features[1] = pos % 128
    key_features[HEAD_DIM + pos // NSA_SEL_LEN, pos] = 1.0
    consts = {"cover": jnp.asarray(cover, BF16), "tile_of_block": jnp.asarray(tile_of_block, BF16),
              "key_features": jnp.asarray(key_features, BF16)}

    outs = []
    for b in range(B):
        h = x[b]
        for i in range(depth):
            w = jnp.pad(w_in[i][:, perm] * colscale, ((0, 0), (0, n_pad))).astype(BF16)
            main, gates, x16, xd4, xd16 = _inproj(h, g_mix[i].reshape(1, D), w, min(512, S))
            cmp_weights = _compress_weights(cmp_pos_k[i], cmp_w1_k[i], cmp_w2_k[i],
                                            cmp_pos_v[i], cmp_w1_v[i], cmp_w2_v[i])
            mixers = (_nsa_mixer(main, x16, gates, b_nsa_gate[i], cmp_weights, slopes_nsa, consts),
                      _dilated_mixer(main, xd4, xd16, slopes_dil),
                      _fox_mixer(main, gates, b_forget[i]),
                      _sb(main, min(SB_TQ, S), min(SB_TK, S)))
            h = _post(h, mixers, p[i, b], g_head[i].reshape(1, D), w_out[i].astype(BF16),
                      g_mlp[i].reshape(1, D), w_up[i].astype(BF16), w_down[i].astype(BF16),
                      g_ple[i].reshape(1, D), w_ple_gate[i].astype(BF16), b_ple_gate[i].reshape(1, D),
                      w_ple_proj[i].astype(BF16), g_final.reshape(1, D), min(256, S), i == depth - 1)
        outs.append(h)
    return jnp.stack(outs)
```

```python
import functools
import math

import numpy as np
import jax
import jax.numpy as jnp
from jax import lax
from jax.experimental import pallas as pl
from jax.experimental.pallas import tpu as pltpu

F32 = jnp.float32
BF16 = jnp.bfloat16

HEAD_DIM = 64
HEADS_PER_MIXER = 4
GROUP_WIDTH = HEADS_PER_MIXER * HEAD_DIM
PAIR = 2 * HEAD_DIM
N_ALIBI_HEADS = 2 * HEADS_PER_MIXER
NSA_CMP_LEN = 32
NSA_CMP_STRIDE = 16
NSA_CMP_HIDDEN = 256
NSA_SEL_LEN = 64
NSA_N_SEL = 16
NSA_WINDOW = 512
DILATED_PATTERNS = ((128, 1), (512, 4), (2048, 16))
RMS_EPS = 1e-6
SEL_FORCE = 1e9
QK_SCALE = HEAD_DIM ** -0.5

N_MAIN = 2944
N_GATE = 128
COL_QA = 0
COL_KVCMP, COL_KVSEL, COL_KVWIN = 256, 384, 512
COL_QB, COL_KB, COL_VB = 640, 896, 1152
COL_QC, COL_KC, COL_VC = 1408, 1664, 1920
COL_QD, COL_KD, COL_VD = 2176, 2432, 2688
GATE_COL_NSA, GATE_COL_F = 0, 12

NSA_CQ = 256
NSA_TQ, NSA_TK = 128, 512
NSA_CHAINS = 2
FOX_TQ, FOX_TK = 512, 512
SB_TQ, SB_TK = 512, 256
BAND_TQ = 512
EXP_ZERO = -104.0

MASKED = -1e9
M_INIT = -1e8
VMEM_LIMIT = 56 * 1024 * 1024


def _alibi_slopes():
    return 2.0 ** (-8.0 * np.arange(1, N_ALIBI_HEADS + 1) / N_ALIBI_HEADS)


def _params(*sem):
    return pltpu.CompilerParams(dimension_semantics=sem, vmem_limit_bytes=VMEM_LIMIT)


def _resident(shape, index_map):
    return pl.BlockSpec(shape, index_map, pipeline_mode=pl.Buffered(1))


def _dot(a, b):
    return jnp.dot(a, b, preferred_element_type=F32)


def _dot_nt(a, b):
    return lax.dot_general(a, b, (((1,), (1,)), ((), ())), preferred_element_type=F32)


def _split3(x):
    hi = x.astype(BF16)
    r = x - hi.astype(F32)
    mid = r.astype(BF16)
    lo = (r - mid.astype(F32)).astype(BF16)
    return hi, mid, lo


def _dot3(x, w01):
    hi, mid, lo = _split3(x)
    return _dot(hi, w01) + _dot(mid, w01) + _dot(lo, w01)


def _rms(x, g):
    return x * lax.rsqrt(jnp.mean(x * x, axis=-1, keepdims=True) + RMS_EPS) * g


def _softplus(z):
    return jnp.maximum(z, 0.0) + jnp.log(1.0 + jnp.exp(-jnp.abs(z)))


def _head_lanes(shape, hh):
    lane = lax.broadcasted_iota(jnp.int32, shape, len(shape) - 1)
    return lane < HEAD_DIM if hh == 0 else lane >= HEAD_DIM


def _inproj_kernel(h_ref, g_ref, w_ref, main_ref, gate_ref, x16_ref, xd4_ref, xd16_ref, rows_ref):
    tm = h_ref.shape[0]
    u = _rms(h_ref[...], g_ref[...]).astype(BF16)
    acc = _dot(u, w_ref[...])
    main_ref[...] = acc[:, :N_MAIN].astype(main_ref.dtype)
    gate_ref[...] = acc[:, N_MAIN:]
    n_pairs = 3 * GROUP_WIDTH // PAIR
    rows_ref[0] = acc[:, COL_KVCMP:COL_KVCMP + PAIR]
    for c in range(n_pairs):
        rows_ref[1 + c] = acc[:, COL_QB + c * PAIR:COL_QB + (c + 1) * PAIR]
    n16 = tm // NSA_CMP_STRIDE
    for l in range(NSA_CMP_STRIDE):
        x16_ref[:, l * PAIR:(l + 1) * PAIR] = rows_ref[0, pl.ds(l, n16, stride=NSA_CMP_STRIDE), :].astype(BF16)
    for out_ref in (xd4_ref, xd16_ref):
        d = out_ref.shape[0]
        for r in range(d):
            for c in range(n_pairs):
                out_ref[r, :, c * PAIR:(c + 1) * PAIR] = rows_ref[1 + c, pl.ds(r, tm // d, stride=d), :].astype(BF16)


def _inproj(h, g, w, tm):
    S, D = h.shape
    n = w.shape[1]
    d4, d16 = DILATED_PATTERNS[1][1], DILATED_PATTERNS[2][1]
    wide = 3 * GROUP_WIDTH
    return pl.pallas_call(
        _inproj_kernel,
        grid=(S // tm,),
        in_specs=[pl.BlockSpec((tm, D), lambda i: (i, 0)),
                  _resident((1, D), lambda i: (0, 0)),
                  _resident((D, n), lambda i: (0, 0))],
        out_specs=[pl.BlockSpec((tm, N_MAIN), lambda i: (i, 0)),
                   pl.BlockSpec((tm, N_GATE), lambda i: (i, 0)),
                   pl.BlockSpec((tm // NSA_CMP_STRIDE, NSA_CMP_STRIDE * PAIR), lambda i: (i, 0)),
                   pl.BlockSpec((d4, tm // d4, wide), lambda i: (0, i, 0)),
                   pl.BlockSpec((d16, tm // d16, wide), lambda i: (0, i, 0))],
        out_shape=[jax.ShapeDtypeStruct((S, N_MAIN), BF16), jax.ShapeDtypeStruct((S, N_GATE), F32),
                   jax.ShapeDtypeStruct((S // NSA_CMP_STRIDE, NSA_CMP_STRIDE * PAIR), BF16),
                   jax.ShapeDtypeStruct((d4, S // d4, wide), BF16),
                   jax.ShapeDtypeStruct((d16, S // d16, wide), BF16)],
        scratch_shapes=[pltpu.VMEM((1 + wide // PAIR, tm, PAIR), F32)],
        compiler_params=_params("parallel"),
        name="inproj",
    )(h, g, w)


def _compress_kernel(x_ref, pos_ref, w1_ref, w2_ref, out_ref):
    x = x_ref[...].astype(F32)
    n = x.shape[0]
    top = _dot((x + pos_ref[0:1, :]).astype(BF16), w1_ref[0])
    bot = _dot((x + pos_ref[1:2, :]).astype(BF16), w1_ref[1])
    hid = top + pltpu.roll(bot, n - 1, 0)
    c = math.sqrt(2.0 / math.pi)
    hid = 0.5 * hid * (1.0 + jnp.tanh(c * (hid + 0.044715 * (hid * hid * hid))))
    out_ref[...] = _dot(hid.astype(BF16), w2_ref[...]).astype(out_ref.dtype)


def _compress(x16, pos, w1, w2):
    n = x16.shape[0]
    return pl.pallas_call(
        _compress_kernel,
        out_shape=jax.ShapeDtypeStruct((n, PAIR), BF16),
        compiler_params=pltpu.CompilerParams(vmem_limit_bytes=VMEM_LIMIT),
        name="nsa_compress",
    )(x16, pos, w1, w2)


def _nsa_cmp_kernel(q_ref, kvc_ref, cover_ref, tile_ref, oc_ref, selb_ref, flag_ref, *, tq, slopes):
    t0 = pl.program_id(0) * tq
    nc = kvc_ref.shape[0]
    nb = cover_ref.shape[1]
    q = q_ref[...]
    zeros = jnp.zeros((tq, HEAD_DIM), BF16)

    def compressed(width):
        tpos = t0 + lax.broadcasted_iota(jnp.int32, (tq, width), 0)
        jidx = lax.broadcasted_iota(jnp.int32, (tq, width), 1)
        mask = jidx * NSA_CMP_STRIDE + (NSA_CMP_LEN - 1) <= tpos
        dist = tpos.astype(F32) - (jidx.astype(F32) * NSA_CMP_STRIDE + (NSA_CMP_LEN - 1) / 2.0)
        kvc = kvc_ref[0:width, :]
        psum = jnp.zeros((tq, width), F32)
        outs = []
        for h in range(HEADS_PER_MIXER):
            qh = jnp.concatenate([q[:, h * HEAD_DIM:(h + 1) * HEAD_DIM], zeros], axis=1)
            s = _dot_nt(qh, kvc) - slopes[h] * dist
            m = jnp.max(jnp.where(mask, s, -jnp.inf), axis=-1, keepdims=True)
            m = jnp.where(m > -jnp.inf, m, 0.0)
            e = jnp.where(mask, jnp.exp(s - m), 0.0)
            den = jnp.sum(e, axis=-1, keepdims=True)
            p = e / jnp.maximum(den, 1e-30)
            outs.append(_dot(p.astype(BF16), kvc)[:, HEAD_DIM:])
            psum = psum + p
        return jnp.concatenate(outs, axis=1), _dot3(psum, cover_ref[0:width, :])

    n_widths = 4 if nc % 1024 == 0 else 1
    step = nc // n_widths
    last_block = jnp.maximum(t0 + tq - NSA_CMP_LEN, 0) // NSA_CMP_STRIDE
    o_cmp, imp = lax.switch(jnp.minimum(last_block // step, n_widths - 1),
                            [functools.partial(compressed, step * (c + 1)) for c in range(n_widths)])
    oc_ref[...] = o_cmp
    bidx = lax.broadcasted_iota(jnp.int32, (tq, nb), 1)
    qpos = t0 + lax.broadcasted_iota(jnp.int32, (tq, nb), 0)
    cur = qpos // NSA_SEL_LEN
    forced = jnp.where(bidx == 0, 1.0, jnp.where(bidx == cur, 1.0, jnp.where(bidx == cur - 1, 1.0, 0.0)))
    elig = bidx * NSA_SEL_LEN <= qpos
    n_forced = jnp.sum(forced, axis=-1, keepdims=True)
    score = jnp.where(forced > 0.0, -jnp.inf, jnp.where(elig, imp, -SEL_FORCE))
    bf = bidx.astype(F32)

    def pick(state, allow=None):
        sel, score = state
        mx = jnp.max(score, axis=-1, keepdims=True)
        first = jnp.min(jnp.where(score == mx, bf, float(nb)), axis=-1, keepdims=True)
        if allow is not None:
            first = jnp.where(allow, first, -1.0)
        hit = bf == first
        return jnp.where(hit, 1.0, sel), jnp.where(hit, -jnp.inf, score)

    state = (forced, score)
    for _ in range(NSA_N_SEL - 3):
        state = pick(state)

    def early(state):
        for short in (1, 2):
            state = pick(state, n_forced <= 3.0 - short)
        return state

    sel, _ = lax.cond(t0 < 2 * NSA_SEL_LEN, early, lambda st: st, state)
    sel = jnp.where(elig, sel, 0.0)
    selb_ref[...] = jnp.where(sel > 0.0, 0.0, MASKED).astype(selb_ref.dtype)
    for part in range(tq // NSA_TQ):
        any_q = jnp.max(sel[part * NSA_TQ:(part + 1) * NSA_TQ], axis=0, keepdims=True).astype(BF16)
        flag_ref[part] = _dot(jnp.broadcast_to(any_q, (8, nb)), tile_ref[...])


def _nsa_cmp(main, kvc, cover, tile_of_block, tq, slopes):
    S = main.shape[0]
    nc = kvc.shape[0]
    nb = cover.shape[1]
    parts = tq // NSA_TQ
    return pl.pallas_call(
        functools.partial(_nsa_cmp_kernel, tq=tq, slopes=slopes),
        grid=(S // tq,),
        in_specs=[pl.BlockSpec((tq, GROUP_WIDTH), lambda i: (i, COL_QA // GROUP_WIDTH)),
                  _resident((nc, PAIR), lambda i: (0, 0)),
                  _resident((nc, nb), lambda i: (0, 0)),
                  _resident((nb, 128), lambda i: (0, 0))],
        out_specs=[pl.BlockSpec((tq, GROUP_WIDTH), lambda i: (i, 0)),
                   pl.BlockSpec((tq, nb), lambda i: (i, 0)),
                   pl.BlockSpec((parts, 8, 128), lambda i: (i, 0, 0))],
        out_shape=[jax.ShapeDtypeStruct((S, GROUP_WIDTH), F32), jax.ShapeDtypeStruct((S, nb), BF16),
                   jax.ShapeDtypeStruct((S // NSA_TQ, 8, 128), F32)],
        compiler_params=_params("parallel"),
        name="nsa_cmp_select",
    )(main, kvc, cover, tile_of_block)


def _nsa_attend_kernel(tiles_ref, count_ref, q_ref, rhs_ref, kvsel_ref, kvwin_ref, selb_ref, ocmp_ref, gate_ref,
                       bgate_ref, o_ref, lhs_ref, *, tq, tk, slopes):
    H = HEADS_PER_MIXER
    S = kvsel_ref.shape[0]
    i = pl.program_id(0)
    t0 = i * tq
    list_base = i * (S // tk)
    rows = H * tq
    nb = selb_ref.shape[1]

    q = q_ref[...]
    selb = selb_ref[...]
    lane = lax.broadcasted_iota(jnp.int32, (tq, HEAD_DIM), 1)
    for h in range(H):
        aug = jnp.where(lane == 0, 128.0 * slopes[h], jnp.where(lane == 1, slopes[h], 0.0)).astype(BF16)
        lhs_ref[h * tq:(h + 1) * tq, 0:PAIR] = jnp.concatenate([q[:, h * HEAD_DIM:(h + 1) * HEAD_DIM], aug], axis=1)
        lhs_ref[h * tq:(h + 1) * tq, PAIR:PAIR + nb] = selb
    group = rows // NSA_CHAINS

    def tile(j, carries, diagonal):
        ks = pl.multiple_of(j * tk, tk)
        rhs = rhs_ref[:, pl.ds(ks, tk)]
        kv = kvsel_ref[pl.ds(ks, tk), :]
        new = []
        for g in range(NSA_CHAINS):
            m_old, l, acc = carries[g]
            s = _dot(lhs_ref[g * group:(g + 1) * group, :], rhs)
            if diagonal:
                qpos = jnp.concatenate([t0 + lax.broadcasted_iota(jnp.int32, (tq, tk), 0)] * (group // tq), axis=0)
                kpos = ks + lax.broadcasted_iota(jnp.int32, (group, tk), 1)
                s = jnp.where(kpos <= qpos, s, MASKED)
            m_new = jnp.maximum(m_old, jnp.max(s, axis=-1, keepdims=True))
            p = jnp.exp(s - m_new)
            alpha = jnp.exp(m_old - m_new)
            l = alpha * l + jnp.sum(p, axis=-1, keepdims=True)
            acc = alpha * acc + _dot(p.astype(BF16), kv)
            new.append((m_new, l, acc))
        return tuple(new)

    init = (jnp.full((group, 1), M_INIT, F32), jnp.zeros((group, 1), F32), jnp.zeros((group, PAIR), F32))
    carries = lax.fori_loop(0, count_ref[i], lambda n, c: tile(tiles_ref[list_base + n], c, False),
                            (init,) * NSA_CHAINS)
    carries = tile((t0 + tq - 1) // tk, carries, True)
    l_all = jnp.concatenate([c[1] for c in carries], axis=0)
    acc_all = jnp.concatenate([c[2] for c in carries], axis=0)
    o_sel = acc_all[:, HEAD_DIM:] / jnp.maximum(l_all, 1e-30)

    back = NSA_WINDOW - 1
    width = NSA_WINDOW + tq
    start = pl.multiple_of(jnp.maximum(t0 - NSA_WINDOW, 0), tq)
    kw = kvwin_ref[pl.ds(start, width), :]
    qz = jnp.where(_head_lanes((rows, PAIR), 0), lhs_ref[:, 0:PAIR], jnp.zeros((rows, PAIR), BF16))
    sw = _dot_nt(qz, kw)
    qpos = jnp.concatenate([t0 + lax.broadcasted_iota(jnp.int32, (tq, width), 0)] * H, axis=0)
    dist = qpos - (start + lax.broadcasted_iota(jnp.int32, (rows, width), 1))
    slope = jnp.concatenate([jnp.full((tq, 1), slopes[h], F32) for h in range(H)], axis=0)
    sw = sw - slope * dist.astype(F32)
    sw = jnp.where(dist >= 0, jnp.where(dist <= back, sw, -jnp.inf), -jnp.inf)
    mw = jnp.max(sw, axis=-1, keepdims=True)
    ew = jnp.exp(sw - mw)
    o_win = _dot(ew.astype(BF16), kw)[:, HEAD_DIM:] / jnp.sum(ew, axis=-1, keepdims=True)

    g = jax.nn.sigmoid(gate_ref[...] + bgate_ref[...])
    ocmp = ocmp_ref[...]
    outs = []
    for h in range(H):
        r0 = h * tq
        c0 = GATE_COL_NSA + h
        outs.append(g[:, c0:c0 + 1] * ocmp[:, h * HEAD_DIM:(h + 1) * HEAD_DIM]
                    + g[:, c0 + H:c0 + H + 1] * o_sel[r0:r0 + tq]
                    + g[:, c0 + 2 * H:c0 + 2 * H + 1] * o_win[r0:r0 + tq])
    o_ref[...] = _head_norm(jnp.concatenate(outs, axis=1))


def _nsa_attend(tiles, counts, main, rhs, selb, ocmp, gates, bgate, tq, tk, slopes):
    S = main.shape[0]
    nb = selb.shape[1]
    rows = HEADS_PER_MIXER * tq
    grid_spec = pltpu.PrefetchScalarGridSpec(
        num_scalar_prefetch=2,
        grid=(S // tq,),
        in_specs=[pl.BlockSpec((tq, GROUP_WIDTH), lambda i, *_: (i, COL_QA // GROUP_WIDTH)),
                  _resident((rhs.shape[0], S), lambda i, *_: (0, 0)),
                  _resident((S, PAIR), lambda i, *_: (0, COL_KVSEL // PAIR)),
                  _resident((S, PAIR), lambda i, *_: (0, COL_KVWIN // PAIR)),
                  pl.BlockSpec((tq, nb), lambda i, *_: (i, 0)),
                  pl.BlockSpec((tq, GROUP_WIDTH), lambda i, *_: (i, 0)),
                  pl.BlockSpec((tq, N_GATE), lambda i, *_: (i, 0)),
                  _resident((1, N_GATE), lambda i, *_: (0, 0))],
        out_specs=pl.BlockSpec((tq, GROUP_WIDTH), lambda i, *_: (i, 0)),
        scratch_shapes=[pltpu.VMEM((rows, PAIR + nb), BF16)],
    )
    return pl.pallas_call(
        functools.partial(_nsa_attend_kernel, tq=tq, tk=tk, slopes=slopes),
        grid_spec=grid_spec,
        out_shape=jax.ShapeDtypeStruct((S, GROUP_WIDTH), F32),
        compiler_params=_params("parallel"),
        name="nsa_attend",
    )(tiles, counts, main, rhs, main, main, selb, ocmp, gates, bgate)


def _band_kernel(q_ref, k_ref, v_ref, dist_ref, o_ref, lse_ref, *, tq, back_pad, slopes, d):
    pr = pl.program_id(1)
    q0 = pl.multiple_of(pl.program_id(2) * tq, tq)
    width = back_pad + tq
    start = pl.multiple_of(jnp.maximum(q0 - back_pad, 0), 128)
    k = k_ref[0, pl.ds(start, width), :]
    v = v_ref[0, pl.ds(start, width), :]
    q = q_ref[0]
    dist = dist_ref[jnp.where(q0 < back_pad, 0, 1)]
    o_pair, lse_pair = [], []
    for hh in range(2):
        slope = jnp.where(pr == 0, slopes[hh], slopes[2 + hh]) * d
        qh = jnp.where(_head_lanes(q.shape, hh), q, jnp.zeros_like(q))
        s = _dot_nt(qh, k) - slope * dist
        m = jnp.max(s, axis=-1, keepdims=True)
        e = jnp.exp(s - m)
        l = jnp.sum(e, axis=-1, keepdims=True)
        o_pair.append(_dot(e.astype(BF16), v) / l)
        lse_pair.append(m + jnp.log(l))
    first = _head_lanes((tq, PAIR), 0)
    o_ref[0] = jnp.where(first, o_pair[0], o_pair[1])
    lse_ref[0] = jnp.where(first, lse_pair[0], lse_pair[1])


def _band(src, cols, d, back, slopes, tq):
    _, L, _ = src.shape
    back_pad = -(-back // 128) * 128
    width = back_pad + tq
    r_ = np.arange(tq)[:, None]
    c_ = np.arange(width)[None, :]
    dist = np.stack([r_ - c_, r_ + back_pad - c_]).astype(np.float32)
    dist = np.where((dist >= 0) & (dist <= back), dist, 1e30)
    cq, ck, cv = cols
    out_spec = pl.BlockSpec((1, tq, PAIR), lambda r, p_, i: (r, i, p_))
    return pl.pallas_call(
        functools.partial(_band_kernel, tq=tq, back_pad=back_pad, slopes=slopes, d=float(d)),
        grid=(d, 2, L // tq),
        in_specs=[pl.BlockSpec((1, tq, PAIR), lambda r, p_, i: (r, i, cq + p_)),
                  pl.BlockSpec((1, L, PAIR), lambda r, p_, i: (r, 0, ck + p_)),
                  pl.BlockSpec((1, L, PAIR), lambda r, p_, i: (r, 0, cv + p_)),
                  _resident((2, tq, width), lambda r, p_, i: (0, 0, 0))],
        out_specs=[out_spec, out_spec],
        out_shape=[jax.ShapeDtypeStruct((d, L, GROUP_WIDTH), F32)] * 2,
        compiler_params=_params("parallel", "parallel", "parallel"),
        name="band_attention",
    )(src, src, src, jnp.asarray(dist, F32))


def _head_norm(o):
    lane = lax.broadcasted_iota(jnp.int32, o.shape, 1)
    o2 = o * o
    scale = jnp.zeros_like(o)
    for h in range(o.shape[1] // HEAD_DIM):
        mine = jnp.logical_and(lane >= h * HEAD_DIM, lane < (h + 1) * HEAD_DIM)
        ms = jnp.sum(jnp.where(mine, o2, 0.0), axis=-1, keepdims=True) * (1.0 / HEAD_DIM)
        scale = jnp.where(mine, lax.rsqrt(ms + RMS_EPS), scale)
    return o * scale


def _dil_combine_kernel(o1_ref, l1_ref, o4_ref, l4_ref, o16_ref, l16_ref, out_ref, o4_s, l4_s, o16_s, l16_s):
    tm, width = out_ref.shape
    n_pairs = width // PAIR
    for src, dst in ((o4_ref, o4_s), (l4_ref, l4_s), (o16_ref, o16_s), (l16_ref, l16_s)):
        d = src.shape[0]
        for r in range(d):
            for c in range(n_pairs):
                dst[c, pl.ds(r, tm // d, stride=d), :] = src[r, :, c * PAIR:(c + 1) * PAIR]
    token_order = lambda s: jnp.concatenate([s[c] for c in range(n_pairs)], axis=1)
    l1, l2, l3 = l1_ref[0], token_order(l4_s), token_order(l16_s)
    m = jnp.maximum(l1, jnp.maximum(l2, l3))
    e1, e2, e3 = jnp.exp(l1 - m), jnp.exp(l2 - m), jnp.exp(l3 - m)
    den = e1 + e2 + e3
    out_ref[...] = _head_norm((e1 / den) * o1_ref[0] + (e2 / den) * token_order(o4_s)
                              + (e3 / den) * token_order(o16_s))


def _dil_combine(outs, lses, tm):
    S, W = outs[0].shape[1:]
    specs = []
    for o in outs:
        d = o.shape[0]
        specs += [pl.BlockSpec((d, tm // d, W), lambda i: (0, i, 0))] * 2
    args = [a for pair in zip(outs, lses) for a in pair]
    return pl.pallas_call(
        _dil_combine_kernel,
        grid=(S // tm,),
        in_specs=specs,
        out_specs=pl.BlockSpec((tm, W), lambda i: (i, 0)),
        out_shape=jax.ShapeDtypeStruct((S, W), F32),
        scratch_shapes=[pltpu.VMEM((W // PAIR, tm, PAIR), F32)] * 4,
        compiler_params=_params("parallel"),
        name="dilated_combine",
    )(*args)


def _logf_cumsum_kernel(f_ref, b_ref, tri_ref, c_ref, carry_ref):
    @pl.when(pl.program_id(0) == 0)
    def _():
        carry_ref[...] = jnp.zeros_like(carry_ref)

    x = f_ref[...] + b_ref[...]
    logf = jnp.minimum(x, 0.0) - jnp.log1p(jnp.exp(-jnp.abs(x)))
    c = _dot3(logf, tri_ref[...]) + carry_ref[:, :1]
    c_ref[...] = c
    carry_ref[...] = jnp.broadcast_to(c[:, -1:], carry_ref.shape)


def _logf_cumsum(f, b, tk):
    R, S = f.shape
    tri = (np.arange(tk)[:, None] <= np.arange(tk)[None, :]).astype(np.float32)
    return pl.pallas_call(
        _logf_cumsum_kernel,
        grid=(S // tk,),
        in_specs=[pl.BlockSpec((R, tk), lambda i: (0, i)),
                  pl.BlockSpec((R, 1), lambda i: (0, 0)),
                  _resident((tk, tk), lambda i: (0, 0))],
        out_specs=pl.BlockSpec((R, tk), lambda i: (0, i)),
        out_shape=jax.ShapeDtypeStruct((R, S), F32),
        scratch_shapes=[pltpu.VMEM((R, 128), F32)],
        compiler_params=_params("arbitrary"),
        name="logf_cumsum",
    )(f, b, jnp.asarray(tri, BF16))


def _pair_key_norms(k_ref, kmax_ref, chunk):
    S = k_ref.shape[0]
    first = _head_lanes((chunk, PAIR), 0)

    def body(ci, best):
        k = k_ref[pl.ds(pl.multiple_of(ci * chunk, chunk), chunk), :].astype(F32)
        sq = k * k
        n0 = jnp.max(jnp.sum(jnp.where(first, sq, 0.0), axis=-1, keepdims=True), axis=0, keepdims=True)
        n1 = jnp.max(jnp.sum(jnp.where(first, 0.0, sq), axis=-1, keepdims=True), axis=0, keepdims=True)
        return jnp.maximum(best[0], n0), jnp.maximum(best[1], n1)

    zero = jnp.zeros((1, 1), F32)
    n0, n1 = lax.fori_loop(0, S // chunk, body, (zero, zero))
    kmax_ref[0:1, :] = jnp.broadcast_to(jnp.sqrt(n0), (1, kmax_ref.shape[1]))
    kmax_ref[1:2, :] = jnp.broadcast_to(jnp.sqrt(n1), (1, kmax_ref.shape[1]))


def _fox_kernel(q_ref, k_ref, v_ref, c_ref, o_ref, kmax_ref, *, tq, tk):
    i = pl.program_id(1)
    first = (i * tq) // tk

    @pl.when(i == 0)
    def _():
        _pair_key_norms(k_ref, kmax_ref, min(1024, k_ref.shape[0]))

    q = q_ref[...]
    row = lax.broadcasted_iota(jnp.int32, (tq, tk), 0) + (i * tq - first * tk)
    col = lax.broadcasted_iota(jnp.int32, (tq, tk), 1)
    qs, reach = [], []
    for hh in range(2):
        qh = jnp.where(_head_lanes(q.shape, hh), q, jnp.zeros_like(q))
        qf = qh.astype(F32)
        qs.append(qh)
        reach.append(jnp.sqrt(jnp.sum(qf * qf, axis=-1, keepdims=True)) * kmax_ref[hh:hh + 1, :1])

    def tile(j, carries, diagonal):
        ks = pl.multiple_of(j * tk, tk)
        k = k_ref[pl.ds(ks, tk), :]
        v = v_ref[pl.ds(ks, tk), :]
        new, gap = [], None
        for hh in range(2):
            m, l, acc = carries[hh]
            c_tile = c_ref[0, hh:hh + 1, pl.ds(ks, tk)]
            s = _dot_nt(qs[hh], k) - c_tile
            if diagonal:
                s = jnp.where(col <= row, s, -jnp.inf)
            m_new = jnp.maximum(m, jnp.max(s, axis=-1, keepdims=True))
            p = jnp.exp(s - m_new)
            alpha = jnp.exp(m - m_new)
            l = alpha * l + jnp.sum(p, axis=-1, keepdims=True)
            acc = alpha * acc + _dot(p.astype(BF16), v)
            new.append((m_new, l, acc))
            g = jnp.max(reach[hh] - c_tile[:, :1] - m_new)
            gap = g if gap is None else jnp.maximum(gap, g)
        return tuple(new), gap

    init = (jnp.full((tq, 1), -jnp.inf, F32), jnp.zeros((tq, 1), F32), jnp.zeros((tq, PAIR), F32))
    carries, gap = tile(first, (init, init), True)

    def more(state):
        j, gap, _ = state
        return jnp.logical_and(j >= 0, gap >= EXP_ZERO - 1.0)

    def body(state):
        j, _, carries = state
        carries, gap = tile(j, carries, False)
        return j - 1, gap, carries

    _, _, carries = lax.while_loop(more, body, (first - 1, gap, carries))
    outs = [acc / l for _, l, acc in carries]
    o_ref[...] = _head_norm(jnp.where(_head_lanes((tq, PAIR), 0), outs[0], outs[1]))


def _pair_specs(S, tq, col_q, col_k, col_v):
    return [pl.BlockSpec((tq, PAIR), lambda p_, i: (i, col_q // PAIR + p_)),
            pl.BlockSpec((S, PAIR), lambda p_, i: (0, col_k // PAIR + p_)),
            pl.BlockSpec((S, PAIR), lambda p_, i: (0, col_v // PAIR + p_))]


def _fox(main, c, tq, tk):
    S = main.shape[0]
    return pl.pallas_call(
        functools.partial(_fox_kernel, tq=tq, tk=tk),
        grid=(2, S // tq),
        in_specs=_pair_specs(S, tq, COL_QC, COL_KC, COL_VC) + [pl.BlockSpec((1, 2, S), lambda p_, i: (p_, 0, 0))],
        out_specs=pl.BlockSpec((tq, PAIR), lambda p_, i: (i, p_)),
        out_shape=jax.ShapeDtypeStruct((S, GROUP_WIDTH), F32),
        scratch_shapes=[pltpu.VMEM((8, 128), F32)],
        compiler_params=_params("parallel", "arbitrary"),
        name="forgetting_attention",
    )(main, main, main, c)


def _sb_kernel(q_ref, k_ref, v_ref, tri_ref, o_ref, *, tq, tk):
    i = pl.program_id(1)
    per_q = tq // tk
    q = q_ref[...]
    tri = tri_ref[...]
    row = lax.broadcasted_iota(jnp.int32, (tq, tk), 0)
    col = lax.broadcasted_iota(jnp.int32, (tq, tk), 1)
    qs = [jnp.where(_head_lanes(q.shape, hh), q, jnp.zeros_like(q)) for hh in range(2)]

    def tile(j, carries, lead):
        ks = pl.multiple_of(j * tk, tk)
        k = k_ref[pl.ds(ks, tk), :]
        v = v_ref[pl.ds(ks, tk), :]
        strict = None if lead is None else col + lead * tk < row
        new, top = [], None
        for hh in range(2):
            run, acc = carries[hh]
            z = _dot_nt(qs[hh], k)
            log_keep = -_softplus(z)
            log_beta = z + log_keep
            if strict is not None:
                log_keep = jnp.where(strict, log_keep, 0.0)
            later = run + _dot3(log_keep, tri)
            a = jnp.exp(log_beta + later)
            if strict is not None:
                a = jnp.where(strict, a, 0.0)
            acc = acc + _dot(a.astype(BF16), v)
            run = run + jnp.sum(log_keep, axis=-1, keepdims=True)
            new.append((run, acc))
            t = jnp.max(run)
            top = t if top is None else jnp.maximum(top, t)
        return tuple(new), top

    init = (jnp.zeros((tq, 1), F32), jnp.zeros((tq, PAIR), F32))
    carries, top = (init, init), None
    for lead in reversed(range(per_q)):
        carries, top = tile(i * per_q + lead, carries, lead)

    def more(state):
        j, top, _ = state
        return jnp.logical_and(j >= 0, top >= EXP_ZERO)

    def body(state):
        j, _, carries = state
        carries, top = tile(j, carries, None)
        return j - 1, top, carries

    _, _, carries = lax.while_loop(more, body, (i * per_q - 1, top, carries))
    o_ref[...] = _head_norm(jnp.where(_head_lanes((tq, PAIR), 0), carries[0][1], carries[1][1]))


def _sb(main, tq, tk):
    S = main.shape[0]
    tri = (np.arange(tk)[:, None] > np.arange(tk)[None, :]).astype(np.float32)
    return pl.pallas_call(
        functools.partial(_sb_kernel, tq=tq, tk=tk),
        grid=(2, S // tq),
        in_specs=_pair_specs(S, tq, COL_QD, COL_KD, COL_VD) + [_resident((tk, tk), lambda p_, i: (0, 0))],
        out_specs=pl.BlockSpec((tq, PAIR), lambda p_, i: (i, p_)),
        out_shape=jax.ShapeDtypeStruct((S, GROUP_WIDTH), F32),
        compiler_params=_params("parallel", "parallel"),
        name="stick_breaking_attention",
    )(main, main, main, jnp.asarray(tri, BF16))


def _post_kernel(h_ref, oa_ref, ob_ref, oc_ref, od_ref, p_ref, ghead_ref, wout_ref, gmlp_ref, wup_ref,
                 wdown_ref, gple_ref, wgate_ref, bgate_ref, wproj_ref, gfinal_ref, out_ref, *, f_chunk, final):
    o = jnp.concatenate([oa_ref[...], ob_ref[...], oc_ref[...], od_ref[...]], axis=1)
    on = o * ghead_ref[...]
    h = h_ref[...] + _dot(on.astype(BF16), wout_ref[...])
    u = _rms(h, gmlp_ref[...]).astype(BF16)
    d_ff = wup_ref.shape[1]
    acc = jnp.zeros(h.shape, F32)
    for f in range(0, d_ff, f_chunk):
        hid = jnp.maximum(_dot(u, wup_ref[:, f:f + f_chunk]), 0.0)
        acc = acc + _dot((hid * hid).astype(BF16), wdown_ref[f:f + f_chunk, :])
    h = h + acc
    gate = jax.nn.sigmoid(_dot(_rms(h, gple_ref[...]).astype(BF16), wgate_ref[...]) + bgate_ref[...])
    h = h + _dot(p_ref[...].astype(BF16), wproj_ref[...]) * gate
    out_ref[...] = _rms(h, gfinal_ref[...]) if final else h


def _post(h, mixers, p, ghead, wout, gmlp, wup, wdown, gple, wgate, bgate, wproj, gfinal, tm, final):
    S, D = h.shape
    row = lambda i: (i, 0)
    fixed = lambda i: (0, 0)
    vec = _resident((1, D), fixed)
    return pl.pallas_call(
        functools.partial(_post_kernel, f_chunk=1024, final=final),
        grid=(S // tm,),
        in_specs=[pl.BlockSpec((tm, D), row)] + [pl.BlockSpec((tm, GROUP_WIDTH), row)] * 4
                 + [pl.BlockSpec((tm, p.shape[1]), row),
                    vec, _resident(wout.shape, fixed),
                    vec, _resident(wup.shape, fixed), _resident(wdown.shape, fixed),
                    vec, _resident(wgate.shape, fixed), vec, _resident(wproj.shape, fixed), vec],
        out_specs=pl.BlockSpec((tm, D), row),
        out_shape=jax.ShapeDtypeStruct((S, D), F32),
        compiler_params=_params("parallel"),
        name="outproj_mlp_ple",
    )(h, *mixers, p, ghead, wout, gmlp, wup, wdown, gple, wgate, bgate, wproj, gfinal)


def _inproj_columns():
    splits = (256, 64, 64, 64, 64, 64, 64, 12, 256, 256, 256, 256, 256, 256, 4, 256, 256, 256)
    offs = np.concatenate([[0], np.cumsum(splits)])
    seg = lambda k: np.arange(offs[k], offs[k + 1])
    main = [0, 1, 2, 3, 4, 5, 6, 8, 9, 10, 11, 12, 13, 15, 16, 17]
    perm = np.concatenate([seg(k) for k in main] + [seg(7), seg(14)])
    scale = np.ones(perm.shape[0], np.float32)
    for start in (COL_QA, COL_QB, COL_QC, COL_QD):
        scale[start:start + GROUP_WIDTH] = QK_SCALE
    return perm, scale


def _compress_weights(pos_k, w1_k, w2_k, pos_v, w1_v, w2_v):
    half = NSA_CMP_LEN // 2

    def expand_rows(wk, wv):
        z = jnp.zeros_like(wk).reshape(half, HEAD_DIM, -1)
        k_rows = jnp.concatenate([wk.reshape(half, HEAD_DIM, -1), z], axis=2)
        v_rows = jnp.concatenate([z, wv.reshape(half, HEAD_DIM, -1)], axis=2)
        return jnp.concatenate([k_rows, v_rows], axis=1).reshape(half * PAIR, -1)

    n_top = half * HEAD_DIM
    w1 = jnp.stack([expand_rows(w1_k[:n_top], w1_v[:n_top]), expand_rows(w1_k[n_top:], w1_v[n_top:])]).astype(BF16)
    pos = jnp.concatenate([pos_k, pos_v], axis=1)
    pos = jnp.stack([pos[:half].reshape(-1), pos[half:].reshape(-1)])
    zk = jnp.zeros_like(w2_k)
    w2 = jnp.concatenate([jnp.concatenate([w2_k, zk], axis=1), jnp.concatenate([zk, w2_v], axis=1)], axis=0)
    return pos, w1, w2.astype(BF16)


def _nsa_mixer(main, x16, gates, b_gate, cmp_weights, slopes, consts):
    S = main.shape[0]
    kvc = _compress(x16, *cmp_weights)
    tq_c = min(NSA_CQ, S)
    tq = min(NSA_TQ, S)
    tk = min(NSA_TK, S)
    o_cmp, selb, flags = _nsa_cmp(main, kvc, consts["cover"], consts["tile_of_block"], tq_c, slopes)
    n_kt = S // tk
    diag_tile = (np.arange(S // tq) * tq + tq - 1) // tk
    wanted = jnp.logical_and(flags[:, 0, :n_kt] > 0.0, np.arange(n_kt)[None, :] < diag_tile[:, None])
    tiles = jnp.argsort(jnp.logical_not(wanted), axis=1, stable=True).astype(jnp.int32).reshape(-1)
    counts = jnp.sum(wanted, axis=1, dtype=jnp.int32)
    rhs = jnp.concatenate([main[:, COL_KVSEL:COL_KVSEL + HEAD_DIM].T, consts["key_features"]], axis=0)
    bgate = jnp.pad(b_gate, (GATE_COL_NSA, N_GATE - GATE_COL_NSA - b_gate.shape[0])).reshape(1, N_GATE)
    return _nsa_attend(tiles, counts, main, rhs, selb, o_cmp, gates, bgate, tq, tk, slopes)


def _dilated_mixer(main, xd4, xd16, slopes):
    S = main.shape[0]
    main_cols = (COL_QB // PAIR, COL_KB // PAIR, COL_VB // PAIR)
    split_cols = (0, GROUP_WIDTH // PAIR, 2 * GROUP_WIDTH // PAIR)
    outs, lses = [], []
    for (window, d), src, cols in zip(DILATED_PATTERNS, (main[None], xd4, xd16), (main_cols, split_cols, split_cols)):
        assert src.shape[0] == d
        o, lse = _band(src, cols, d, window // d, slopes, min(BAND_TQ, S // d // 2))
        outs.append(o)
        lses.append(lse)
    return _dil_combine(outs, lses, min(512, S))


def _fox_mixer(main, gates, b_f):
    S = main.shape[0]
    H = HEADS_PER_MIXER
    f = gates[:, GATE_COL_F:GATE_COL_F + H].T
    f = jnp.concatenate([f, jnp.zeros_like(f)], axis=0)
    b = jnp.concatenate([b_f, jnp.zeros_like(b_f)]).reshape(2 * H, 1)
    c = _logf_cumsum(f, b, min(512, S))[:H].reshape(2, 2, S)
    return _fox(main, c, min(FOX_TQ, S), min(FOX_TK, S))


def kernel(x, p, g_mix, w_in, b_nsa_gate, b_forget, cmp_pos_k, cmp_w1_k, cmp_w2_k, cmp_pos_v, cmp_w1_v, cmp_w2_v,
           g_head, w_out, g_mlp, w_up, w_down, g_ple, w_ple_gate, b_ple_gate, w_ple_proj, g_final):
    B, S, D = x.shape
    depth = w_in.shape[0]
    slopes = _alibi_slopes()
    slopes_nsa = tuple(float(s) for s in slopes[0::2])
    slopes_dil = tuple(float(s) for s in slopes[1::2])
    perm, colscale = _inproj_columns()
    n_pad = N_MAIN + N_GATE - perm.shape[0]

    n_blk = S // NSA_SEL_LEN
    n_cmp_pad = S // NSA_CMP_STRIDE
    cstart = np.arange(n_cmp_pad) * NSA_CMP_STRIDE
    bstart = np.arange(n_blk) * NSA_SEL_LEN
    cover = (cstart[:, None] < bstart[None, :] + NSA_SEL_LEN) & (cstart[:, None] + NSA_CMP_LEN - 1 >= bstart[None, :])
    tile_of_block = bstart[:, None] // min(NSA_TK, S) == np.arange(128)[None, :]
    pos = np.arange(S)
    key_features = np.zeros((HEAD_DIM + n_blk, S), np.float32)
    key_features[0] = pos // 128
    key_features[1] = pos % 128
    key_features[HEAD_DIM + pos // NSA_SEL_LEN, pos] = 1.0
    consts = {"cover": jnp.asarray(cover, BF16), "tile_of_block": jnp.asarray(tile_of_block, BF16),
              "key_features": jnp.asarray(key_features, BF16)}

    outs = []
    for b in range(B):
        h = x[b]
        for i in range(depth):
            w = jnp.pad(w_in[i][:, perm] * colscale, ((0, 0), (0, n_pad))).astype(BF16)
            main, gates, x16, xd4, xd16 = _inproj(h, g_mix[i].reshape(1, D), w, min(512, S))
            cmp_weights = _compress_weights(cmp_pos_k[i], cmp_w1_k[i], cmp_w2_k[i],
                                            cmp_pos_v[i], cmp_w1_v[i], cmp_w2_v[i])
            mixers = (_nsa_mixer(main, x16, gates, b_nsa_gate[i], cmp_weights, slopes_nsa, consts),
                      _dilated_mixer(main, xd4, xd16, slopes_dil),
                      _fox_mixer(main, gates, b_forget[i]),
                      _sb(main, min(SB_TQ, S), min(SB_TK, S)))
            h = _post(h, mixers, p[i, b], g_head[i].reshape(1, D), w_out[i].astype(BF16),
                      g_mlp[i].reshape(1, D), w_up[i].astype(BF16), w_down[i].astype(BF16),
                      g_ple[i].reshape(1, D), w_ple_gate[i].astype(BF16), b_ple_gate[i].reshape(1, D),
                      w_ple_proj[i].astype(BF16), g_final.reshape(1, D), min(256, S), i == depth - 1)
        outs.append(h)
    return jnp.stack(outs)
```

```python
import functools
import math

import numpy as np
import jax
import jax.numpy as jnp
from jax import lax
from jax.experimental import pallas as pl
from jax.experimental.pallas import tpu as pltpu

F32 = jnp.float32
BF16 = jnp.bfloat16

HEAD_DIM = 64
HEADS_PER_MIXER = 4
GROUP_WIDTH = HEADS_PER_MIXER * HEAD_DIM
PAIR = 2 * HEAD_DIM
N_ALIBI_HEADS = 2 * HEADS_PER_MIXER
NSA_CMP_LEN = 32
NSA_CMP_STRIDE = 16
NSA_CMP_HIDDEN = 256
NSA_SEL_LEN = 64
NSA_N_SEL = 16
NSA_WINDOW = 512
DILATED_PATTERNS = ((128, 1), (512, 4), (2048, 16))
RMS_EPS = 1e-6
SEL_FORCE = 1e9
QK_SCALE = HEAD_DIM ** -0.5

N_MAIN = 2944
N_GATE = 128
COL_QA = 0
COL_KVCMP, COL_KVSEL, COL_KVWIN = 256, 384, 512
COL_QB, COL_KB, COL_VB = 640, 896, 1152
COL_QC, COL_KC, COL_VC = 1408, 1664, 1920
COL_QD, COL_KD, COL_VD = 2176, 2432, 2688
GATE_COL_NSA, GATE_COL_F = 0, 12

NSA_CQ = 256
NSA_TQ, NSA_TK = 128, 512
NSA_CHAINS = 2
FOX_TQ, FOX_TK = 512, 512
SB_TQ, SB_TK = 512, 256
BAND_TQ = 2048
EXP_ZERO = -104.0

MASKED = -1e9
M_INIT = -1e8
VMEM_LIMIT = 56 * 1024 * 1024


def _alibi_slopes():
    return 2.0 ** (-8.0 * np.arange(1, N_ALIBI_HEADS + 1) / N_ALIBI_HEADS)


def _params(*sem):
    return pltpu.CompilerParams(dimension_semantics=sem, vmem_limit_bytes=VMEM_LIMIT)


def _resident(shape, index_map):
    return pl.BlockSpec(shape, index_map, pipeline_mode=pl.Buffered(1))


def _dot(a, b):
    return jnp.dot(a, b, preferred_element_type=F32)


def _dot_nt(a, b):
    return lax.dot_general(a, b, (((1,), (1,)), ((), ())), preferred_element_type=F32)


def _split3(x):
    hi = x.astype(BF16)
    r = x - hi.astype(F32)
    mid = r.astype(BF16)
    lo = (r - mid.astype(F32)).astype(BF16)
    return hi, mid, lo


def _dot3(x, w01):
    hi, mid, lo = _split3(x)
    return _dot(hi, w01) + _dot(mid, w01) + _dot(lo, w01)


def _rms(x, g):
    return x * lax.rsqrt(jnp.mean(x * x, axis=-1, keepdims=True) + RMS_EPS) * g


def _softplus(z):
    return jnp.maximum(z, 0.0) + jnp.log(1.0 + jnp.exp(-jnp.abs(z)))


def _head_lanes(shape, hh):
    lane = lax.broadcasted_iota(jnp.int32, shape, len(shape) - 1)
    return lane < HEAD_DIM if hh == 0 else lane >= HEAD_DIM


def _inproj_kernel(h_ref, g_ref, w_ref, main_ref, gate_ref, x16_ref, xd4_ref, xd16_ref, rows_ref):
    tm = h_ref.shape[0]
    u = _rms(h_ref[...], g_ref[...]).astype(BF16)
    acc = _dot(u, w_ref[...])
    main_ref[...] = acc[:, :N_MAIN].astype(main_ref.dtype)
    gate_ref[...] = acc[:, N_MAIN:]
    n_pairs = 3 * GROUP_WIDTH // PAIR
    rows_ref[0] = acc[:, COL_KVCMP:COL_KVCMP + PAIR]
    for c in range(n_pairs):
        rows_ref[1 + c] = acc[:, COL_QB + c * PAIR:COL_QB + (c + 1) * PAIR]
    n16 = tm // NSA_CMP_STRIDE
    for l in range(NSA_CMP_STRIDE):
        x16_ref[:, l * PAIR:(l + 1) * PAIR] = rows_ref[0, pl.ds(l, n16, stride=NSA_CMP_STRIDE), :].astype(BF16)
    for out_ref in (xd4_ref, xd16_ref):
        d = out_ref.shape[0]
        for r in range(d):
            for c in range(n_pairs):
                out_ref[r, :, c * PAIR:(c + 1) * PAIR] = rows_ref[1 + c, pl.ds(r, tm // d, stride=d), :].astype(BF16)


def _inproj(h, g, w, tm):
    S, D = h.shape
    n = w.shape[1]
    d4, d16 = DILATED_PATTERNS[1][1], DILATED_PATTERNS[2][1]
    wide = 3 * GROUP_WIDTH
    return pl.pallas_call(
        _inproj_kernel,
        grid=(S // tm,),
        in_specs=[pl.BlockSpec((tm, D), lambda i: (i, 0)),
                  _resident((1, D), lambda i: (0, 0)),
                  _resident((D, n), lambda i: (0, 0))],
        out_specs=[pl.BlockSpec((tm, N_MAIN), lambda i: (i, 0)),
                   pl.BlockSpec((tm, N_GATE), lambda i: (i, 0)),
                   pl.BlockSpec((tm // NSA_CMP_STRIDE, NSA_CMP_STRIDE * PAIR), lambda i: (i, 0)),
                   pl.BlockSpec((d4, tm // d4, wide), lambda i: (0, i, 0)),
                   pl.BlockSpec((d16, tm // d16, wide), lambda i: (0, i, 0))],
        out_shape=[jax.ShapeDtypeStruct((S, N_MAIN), BF16), jax.ShapeDtypeStruct((S, N_GATE), F32),
                   jax.ShapeDtypeStruct((S // NSA_CMP_STRIDE, NSA_CMP_STRIDE * PAIR), BF16),
                   jax.ShapeDtypeStruct((d4, S // d4, wide), BF16),
                   jax.ShapeDtypeStruct((d16, S // d16, wide), BF16)],
        scratch_shapes=[pltpu.VMEM((1 + wide // PAIR, tm, PAIR), F32)],
        compiler_params=_params("parallel"),
        name="inproj",
    )(h, g, w)


def _compress_kernel(x_ref, pos_ref, w1_ref, w2_ref, out_ref):
    x = x_ref[...].astype(F32)
    n = x.shape[0]
    top = _dot((x + pos_ref[0:1, :]).astype(BF16), w1_ref[0])
    bot = _dot((x + pos_ref[1:2, :]).astype(BF16), w1_ref[1])
    hid = top + pltpu.roll(bot, n - 1, 0)
    c = math.sqrt(2.0 / math.pi)
    hid = 0.5 * hid * (1.0 + jnp.tanh(c * (hid + 0.044715 * (hid * hid * hid))))
    out_ref[...] = _dot(hid.astype(BF16), w2_ref[...]).astype(out_ref.dtype)


def _compress(x16, pos, w1, w2):
    n = x16.shape[0]
    return pl.pallas_call(
        _compress_kernel,
        out_shape=jax.ShapeDtypeStruct((n, PAIR), BF16),
        compiler_params=pltpu.CompilerParams(vmem_limit_bytes=VMEM_LIMIT),
        name="nsa_compress",
    )(x16, pos, w1, w2)


def _nsa_cmp_kernel(q_ref, kvc_ref, cover_ref, tile_ref, oc_ref, selb_ref, flag_ref, *, tq, slopes):
    t0 = pl.program_id(0) * tq
    nc = kvc_ref.shape[0]
    nb = cover_ref.shape[1]
    q = q_ref[...]
    zeros = jnp.zeros((tq, HEAD_DIM), BF16)

    def compressed(width):
        tpos = t0 + lax.broadcasted_iota(jnp.int32, (tq, width), 0)
        jidx = lax.broadcasted_iota(jnp.int32, (tq, width), 1)
        mask = jidx * NSA_CMP_STRIDE + (NSA_CMP_LEN - 1) <= tpos
        dist = tpos.astype(F32) - (jidx.astype(F32) * NSA_CMP_STRIDE + (NSA_CMP_LEN - 1) / 2.0)
        kvc = kvc_ref[0:width, :]
        psum = jnp.zeros((tq, width), F32)
        outs = []
        for h in range(HEADS_PER_MIXER):
            qh = jnp.concatenate([q[:, h * HEAD_DIM:(h + 1) * HEAD_DIM], zeros], axis=1)
            s = _dot_nt(qh, kvc) - slopes[h] * dist
            m = jnp.max(jnp.where(mask, s, -jnp.inf), axis=-1, keepdims=True)
            m = jnp.where(m > -jnp.inf, m, 0.0)
            e = jnp.where(mask, jnp.exp(s - m), 0.0)
            den = jnp.sum(e, axis=-1, keepdims=True)
            p = e / jnp.maximum(den, 1e-30)
            outs.append(_dot(p.astype(BF16), kvc)[:, HEAD_DIM:])
            psum = psum + p
        return jnp.concatenate(outs, axis=1), _dot3(psum, cover_ref[0:width, :])

    n_widths = 4 if nc % 1024 == 0 else 1
    step = nc // n_widths
    last_block = jnp.maximum(t0 + tq - NSA_CMP_LEN, 0) // NSA_CMP_STRIDE
    o_cmp, imp = lax.switch(jnp.minimum(last_block // step, n_widths - 1),
                            [functools.partial(compressed, step * (c + 1)) for c in range(n_widths)])
    oc_ref[...] = o_cmp
    bidx = lax.broadcasted_iota(jnp.int32, (tq, nb), 1)
    qpos = t0 + lax.broadcasted_iota(jnp.int32, (tq, nb), 0)
    cur = qpos // NSA_SEL_LEN
    forced = jnp.where(bidx == 0, 1.0, jnp.where(bidx == cur, 1.0, jnp.where(bidx == cur - 1, 1.0, 0.0)))
    elig = bidx * NSA_SEL_LEN <= qpos
    n_forced = jnp.sum(forced, axis=-1, keepdims=True)
    score = jnp.where(forced > 0.0, -jnp.inf, jnp.where(elig, imp, -SEL_FORCE))
    bf = bidx.astype(F32)

    def pick(state, allow=None):
        sel, score = state
        mx = jnp.max(score, axis=-1, keepdims=True)
        first = jnp.min(jnp.where(score == mx, bf, float(nb)), axis=-1, keepdims=True)
        if allow is not None:
            first = jnp.where(allow, first, -1.0)
        hit = bf == first
        return jnp.where(hit, 1.0, sel), jnp.where(hit, -jnp.inf, score)

    state = (forced, score)
    for _ in range(NSA_N_SEL - 3):
        state = pick(state)

    def early(state):
        for short in (1, 2):
            state = pick(state, n_forced <= 3.0 - short)
        return state

    sel, _ = lax.cond(t0 < 2 * NSA_SEL_LEN, early, lambda st: st, state)
    sel = jnp.where(elig, sel, 0.0)
    selb_ref[...] = jnp.where(sel > 0.0, 0.0, MASKED).astype(selb_ref.dtype)
    for part in range(tq // NSA_TQ):
        any_q = jnp.max(sel[part * NSA_TQ:(part + 1) * NSA_TQ], axis=0, keepdims=True).astype(BF16)
        flag_ref[part] = _dot(jnp.broadcast_to(any_q, (8, nb)), tile_ref[...])


def _nsa_cmp(main, kvc, cover, tile_of_block, tq, slopes):
    S = main.shape[0]
    nc = kvc.shape[0]
    nb = cover.shape[1]
    parts = tq // NSA_TQ
    return pl.pallas_call(
        functools.partial(_nsa_cmp_kernel, tq=tq, slopes=slopes),
        grid=(S // tq,),
        in_specs=[pl.BlockSpec((tq, GROUP_WIDTH), lambda i: (i, COL_QA // GROUP_WIDTH)),
                  _resident((nc, PAIR), lambda i: (0, 0)),
                  _resident((nc, nb), lambda i: (0, 0)),
                  _resident((nb, 128), lambda i: (0, 0))],
        out_specs=[pl.BlockSpec((tq, GROUP_WIDTH), lambda i: (i, 0)),
                   pl.BlockSpec((tq, nb), lambda i: (i, 0)),
                   pl.BlockSpec((parts, 8, 128), lambda i: (i, 0, 0))],
        out_shape=[jax.ShapeDtypeStruct((S, GROUP_WIDTH), F32), jax.ShapeDtypeStruct((S, nb), BF16),
                   jax.ShapeDtypeStruct((S // NSA_TQ, 8, 128), F32)],
        compiler_params=_params("parallel"),
        name="nsa_cmp_select",
    )(main, kvc, cover, tile_of_block)


def _nsa_attend_kernel(tiles_ref, count_ref, q_ref, rhs_ref, kvsel_ref, kvwin_ref, selb_ref, ocmp_ref, gate_ref,
                       bgate_ref, o_ref, lhs_ref, *, tq, tk, slopes):
    H = HEADS_PER_MIXER
    S = kvsel_ref.shape[0]
    i = pl.program_id(0)
    t0 = i * tq
    list_base = i * (S // tk)
    rows = H * tq
    nb = selb_ref.shape[1]

    q = q_ref[...]
    selb = selb_ref[...]
    lane = lax.broadcasted_iota(jnp.int32, (tq, HEAD_DIM), 1)
    for h in range(H):
        aug = jnp.where(lane == 0, 128.0 * slopes[h], jnp.where(lane == 1, slopes[h], 0.0)).astype(BF16)
        lhs_ref[h * tq:(h + 1) * tq, 0:PAIR] = jnp.concatenate([q[:, h * HEAD_DIM:(h + 1) * HEAD_DIM], aug], axis=1)
        lhs_ref[h * tq:(h + 1) * tq, PAIR:PAIR + nb] = selb
    group = rows // NSA_CHAINS

    def tile(j, carries, diagonal):
        ks = pl.multiple_of(j * tk, tk)
        rhs = rhs_ref[:, pl.ds(ks, tk)]
        kv = kvsel_ref[pl.ds(ks, tk), :]
        new = []
        for g in range(NSA_CHAINS):
            m_old, l, acc = carries[g]
            s = _dot(lhs_ref[g * group:(g + 1) * group, :], rhs)
            if diagonal:
                qpos = jnp.concatenate([t0 + lax.broadcasted_iota(jnp.int32, (tq, tk), 0)] * (group // tq), axis=0)
                kpos = ks + lax.broadcasted_iota(jnp.int32, (group, tk), 1)
                s = jnp.where(kpos <= qpos, s, MASKED)
            m_new = jnp.maximum(m_old, jnp.max(s, axis=-1, keepdims=True))
            p = jnp.exp(s - m_new)
            alpha = jnp.exp(m_old - m_new)
            l = alpha * l + jnp.sum(p, axis=-1, keepdims=True)
            acc = alpha * acc + _dot(p.astype(BF16), kv)
            new.append((m_new, l, acc))
        return tuple(new)

    init = (jnp.full((group, 1), M_INIT, F32), jnp.zeros((group, 1), F32), jnp.zeros((group, PAIR), F32))
    carries = lax.fori_loop(0, count_ref[i], lambda n, c: tile(tiles_ref[list_base + n], c, False),
                            (init,) * NSA_CHAINS)
    carries = tile((t0 + tq - 1) // tk, carries, True)
    l_all = jnp.concatenate([c[1] for c in carries], axis=0)
    acc_all = jnp.concatenate([c[2] for c in carries], axis=0)
    o_sel = acc_all[:, HEAD_DIM:] / jnp.maximum(l_all, 1e-30)

    back = NSA_WINDOW - 1
    width = NSA_WINDOW + tq
    start = pl.multiple_of(jnp.maximum(t0 - NSA_WINDOW, 0), tq)
    kw = kvwin_ref[pl.ds(start, width), :]
    qz = jnp.where(_head_lanes((rows, PAIR), 0), lhs_ref[:, 0:PAIR], jnp.zeros((rows, PAIR), BF16))
    sw = _dot_nt(qz, kw)
    qpos = jnp.concatenate([t0 + lax.broadcasted_iota(jnp.int32, (tq, width), 0)] * H, axis=0)
    dist = qpos - (start + lax.broadcasted_iota(jnp.int32, (rows, width), 1))
    slope = jnp.concatenate([jnp.full((tq, 1), slopes[h], F32) for h in range(H)], axis=0)
    sw = sw - slope * dist.astype(F32)
    sw = jnp.where(dist >= 0, jnp.where(dist <= back, sw, -jnp.inf), -jnp.inf)
    mw = jnp.max(sw, axis=-1, keepdims=True)
    ew = jnp.exp(sw - mw)
    o_win = _dot(ew.astype(BF16), kw)[:, HEAD_DIM:] / jnp.sum(ew, axis=-1, keepdims=True)

    g = jax.nn.sigmoid(gate_ref[...] + bgate_ref[...])
    ocmp = ocmp_ref[...]
    outs = []
    for h in range(H):
        r0 = h * tq
        c0 = GATE_COL_NSA + h
        outs.append(g[:, c0:c0 + 1] * ocmp[:, h * HEAD_DIM:(h + 1) * HEAD_DIM]
                    + g[:, c0 + H:c0 + H + 1] * o_sel[r0:r0 + tq]
                    + g[:, c0 + 2 * H:c0 + 2 * H + 1] * o_win[r0:r0 + tq])
    o_ref[...] = _head_norm(jnp.concatenate(outs, axis=1))


def _nsa_attend(tiles, counts, main, rhs, selb, ocmp, gates, bgate, tq, tk, slopes):
    S = main.shape[0]
    nb = selb.shape[1]
    rows = HEADS_PER_MIXER * tq
    grid_spec = pltpu.PrefetchScalarGridSpec(
        num_scalar_prefetch=2,
        grid=(S // tq,),
        in_specs=[pl.BlockSpec((tq, GROUP_WIDTH), lambda i, *_: (i, COL_QA // GROUP_WIDTH)),
                  _resident((rhs.shape[0], S), lambda i, *_: (0, 0)),
                  _resident((S, PAIR), lambda i, *_: (0, COL_KVSEL // PAIR)),
                  _resident((S, PAIR), lambda i, *_: (0, COL_KVWIN // PAIR)),
                  pl.BlockSpec((tq, nb), lambda i, *_: (i, 0)),
                  pl.BlockSpec((tq, GROUP_WIDTH), lambda i, *_: (i, 0)),
                  pl.BlockSpec((tq, N_GATE), lambda i, *_: (i, 0)),
                  _resident((1, N_GATE), lambda i, *_: (0, 0))],
        out_specs=pl.BlockSpec((tq, GROUP_WIDTH), lambda i, *_: (i, 0)),
        scratch_shapes=[pltpu.VMEM((rows, PAIR + nb), BF16)],
    )
    return pl.pallas_call(
        functools.partial(_nsa_attend_kernel, tq=tq, tk=tk, slopes=slopes),
        grid_spec=grid_spec,
        out_shape=jax.ShapeDtypeStruct((S, GROUP_WIDTH), F32),
        compiler_params=_params("parallel"),
        name="nsa_attend",
    )(tiles, counts, main, rhs, main, main, selb, ocmp, gates, bgate)


def _band_kernel(q_ref, k_ref, v_ref, dist_ref, o_ref, lse_ref, *, tq, back_pad, slopes, d):
    pr = pl.program_id(1)
    q0 = pl.multiple_of(pl.program_id(2) * tq, tq)
    sub = back_pad
    width = 2 * back_pad
    first = _head_lanes((sub, PAIR), 0)
    for b in range(tq // sub):
        r0 = q0 + b * sub
        start = pl.multiple_of(jnp.maximum(r0 - back_pad, 0), 128)
        k = k_ref[0, pl.ds(start, width), :]
        v = v_ref[0, pl.ds(start, width), :]
        q = q_ref[0, b * sub:(b + 1) * sub, :]
        dist = dist_ref[jnp.where(r0 < back_pad, 0, 1)]
        o_pair, lse_pair = [], []
        for hh in range(2):
            slope = jnp.where(pr == 0, slopes[hh], slopes[2 + hh]) * d
            qh = jnp.where(_head_lanes(q.shape, hh), q, jnp.zeros_like(q))
            s = _dot_nt(qh, k) - slope * dist
            m = jnp.max(s, axis=-1, keepdims=True)
            e = jnp.exp(s - m)
            l = jnp.sum(e, axis=-1, keepdims=True)
            o_pair.append(_dot(e.astype(BF16), v) / l)
            lse_pair.append(m + jnp.log(l))
        o_ref[0, b * sub:(b + 1) * sub, :] = jnp.where(first, o_pair[0], o_pair[1])
        lse_ref[0, b * sub:(b + 1) * sub, :] = jnp.where(first, lse_pair[0], lse_pair[1])


def _band(src, cols, d, back, slopes, tq):
    _, L, _ = src.shape
    back_pad = -(-back // 128) * 128
    assert tq % back_pad == 0 and L >= 2 * back_pad
    width = 2 * back_pad
    r_ = np.arange(back_pad)[:, None]
    c_ = np.arange(width)[None, :]
    dist = np.stack([r_ - c_, r_ + back_pad - c_]).astype(np.float32)
    dist = np.where((dist >= 0) & (dist <= back), dist, 1e30)
    cq, ck, cv = cols
    out_spec = pl.BlockSpec((1, tq, PAIR), lambda r, p_, i: (r, i, p_))
    return pl.pallas_call(
        functools.partial(_band_kernel, tq=tq, back_pad=back_pad, slopes=slopes, d=float(d)),
        grid=(d, 2, L // tq),
        in_specs=[pl.BlockSpec((1, tq, PAIR), lambda r, p_, i: (r, i, cq + p_)),
                  pl.BlockSpec((1, L, PAIR), lambda r, p_, i: (r, 0, ck + p_)),
                  pl.BlockSpec((1, L, PAIR), lambda r, p_, i: (r, 0, cv + p_)),
                  _resident((2, back_pad, width), lambda r, p_, i: (0, 0, 0))],
        out_specs=[out_spec, out_spec],
        out_shape=[jax.ShapeDtypeStruct((d, L, GROUP_WIDTH), F32)] * 2,
        compiler_params=_params("parallel", "parallel", "parallel"),
        name="band_attention",
    )(src, src, src, jnp.asarray(dist, F32))


def _head_norm(o):
    lane = lax.broadcasted_iota(jnp.int32, o.shape, 1)
    o2 = o * o
    scale = jnp.zeros_like(o)
    for h in range(o.shape[1] // HEAD_DIM):
        mine = jnp.logical_and(lane >= h * HEAD_DIM, lane < (h + 1) * HEAD_DIM)
        ms = jnp.sum(jnp.where(mine, o2, 0.0), axis=-1, keepdims=True) * (1.0 / HEAD_DIM)
        scale = jnp.where(mine, lax.rsqrt(ms + RMS_EPS), scale)
    return o * scale


def _dil_combine_kernel(o1_ref, l1_ref, o4_ref, l4_ref, o16_ref, l16_ref, out_ref, o4_s, l4_s, o16_s, l16_s):
    tm, width = out_ref.shape
    n_pairs = width // PAIR
    for src, dst in ((o4_ref, o4_s), (l4_ref, l4_s), (o16_ref, o16_s), (l16_ref, l16_s)):
        d = src.shape[0]
        for r in range(d):
            for c in range(n_pairs):
                dst[c, pl.ds(r, tm // d, stride=d), :] = src[r, :, c * PAIR:(c + 1) * PAIR]
    token_order = lambda s: jnp.concatenate([s[c] for c in range(n_pairs)], axis=1)
    l1, l2, l3 = l1_ref[0], token_order(l4_s), token_order(l16_s)
    m = jnp.maximum(l1, jnp.maximum(l2, l3))
    e1, e2, e3 = jnp.exp(l1 - m), jnp.exp(l2 - m), jnp.exp(l3 - m)
    den = e1 + e2 + e3
    out_ref[...] = _head_norm((e1 / den) * o1_ref[0] + (e2 / den) * token_order(o4_s)
                              + (e3 / den) * token_order(o16_s))


def _dil_combine(outs, lses, tm):
    S, W = outs[0].shape[1:]
    specs = []
    for o in outs:
        d = o.shape[0]
        specs += [pl.BlockSpec((d, tm // d, W), lambda i: (0, i, 0))] * 2
    args = [a for pair in zip(outs, lses) for a in pair]
    return pl.pallas_call(
        _dil_combine_kernel,
        grid=(S // tm,),
        in_specs=specs,
        out_specs=pl.BlockSpec((tm, W), lambda i: (i, 0)),
        out_shape=jax.ShapeDtypeStruct((S, W), F32),
        scratch_shapes=[pltpu.VMEM((W // PAIR, tm, PAIR), F32)] * 4,
        compiler_params=_params("parallel"),
        name="dilated_combine",
    )(*args)


def _logf_cumsum_kernel(f_ref, b_ref, tri_ref, c_ref, carry_ref):
    @pl.when(pl.program_id(0) == 0)
    def _():
        carry_ref[...] = jnp.zeros_like(carry_ref)

    x = f_ref[...] + b_ref[...]
    logf = jnp.minimum(x, 0.0) - jnp.log1p(jnp.exp(-jnp.abs(x)))
    c = _dot3(logf, tri_ref[...]) + carry_ref[:, :1]
    c_ref[...] = c
    carry_ref[...] = jnp.broadcast_to(c[:, -1:], carry_ref.shape)


def _logf_cumsum(f, b, tk):
    R, S = f.shape
    tri = (np.arange(tk)[:, None] <= np.arange(tk)[None, :]).astype(np.float32)
    return pl.pallas_call(
        _logf_cumsum_kernel,
        grid=(S // tk,),
        in_specs=[pl.BlockSpec((R, tk), lambda i: (0, i)),
                  pl.BlockSpec((R, 1), lambda i: (0, 0)),
                  _resident((tk, tk), lambda i: (0, 0))],
        out_specs=pl.BlockSpec((R, tk), lambda i: (0, i)),
        out_shape=jax.ShapeDtypeStruct((R, S), F32),
        scratch_shapes=[pltpu.VMEM((R, 128), F32)],
        compiler_params=_params("arbitrary"),
        name="logf_cumsum",
    )(f, b, jnp.asarray(tri, BF16))


def _pair_key_norms(k_ref, kmax_ref, chunk):
    S = k_ref.shape[0]
    first = _head_lanes((chunk, PAIR), 0)

    def body(ci, best):
        k = k_ref[pl.ds(pl.multiple_of(ci * chunk, chunk), chunk), :].astype(F32)
        sq = k * k
        n0 = jnp.max(jnp.sum(jnp.where(first, sq, 0.0), axis=-1, keepdims=True), axis=0, keepdims=True)
        n1 = jnp.max(jnp.sum(jnp.where(first, 0.0, sq), axis=-1, keepdims=True), axis=0, keepdims=True)
        return jnp.maximum(best[0], n0), jnp.maximum(best[1], n1)

    zero = jnp.zeros((1, 1), F32)
    n0, n1 = lax.fori_loop(0, S // chunk, body, (zero, zero))
    kmax_ref[0:1, :] = jnp.broadcast_to(jnp.sqrt(n0), (1, kmax_ref.shape[1]))
    kmax_ref[1:2, :] = jnp.broadcast_to(jnp.sqrt(n1), (1, kmax_ref.shape[1]))


def _fox_kernel(q_ref, k_ref, v_ref, c_ref, o_ref, kmax_ref, *, tq, tk):
    i = pl.program_id(1)
    first = (i * tq) // tk

    @pl.when(i == 0)
    def _():
        _pair_key_norms(k_ref, kmax_ref, min(1024, k_ref.shape[0]))

    q = q_ref[...]
    row = lax.broadcasted_iota(jnp.int32, (tq, tk), 0) + (i * tq - first * tk)
    col = lax.broadcasted_iota(jnp.int32, (tq, tk), 1)
    qs, reach = [], []
    for hh in range(2):
        qh = jnp.where(_head_lanes(q.shape, hh), q, jnp.zeros_like(q))
        qf = qh.astype(F32)
        qs.append(qh)
        reach.append(jnp.sqrt(jnp.sum(qf * qf, axis=-1, keepdims=True)) * kmax_ref[hh:hh + 1, :1])

    def tile(j, carries, diagonal):
        ks = pl.multiple_of(j * tk, tk)
        k = k_ref[pl.ds(ks, tk), :]
        v = v_ref[pl.ds(ks, tk), :]
        new, gap = [], None
        for hh in range(2):
            m, l, acc = carries[hh]
            c_tile = c_ref[0, hh:hh + 1, pl.ds(ks, tk)]
            s = _dot_nt(qs[hh], k) - c_tile
            if diagonal:
                s = jnp.where(col <= row, s, -jnp.inf)
            m_new = jnp.maximum(m, jnp.max(s, axis=-1, keepdims=True))
            p = jnp.exp(s - m_new)
            alpha = jnp.exp(m - m_new)
            l = alpha * l + jnp.sum(p, axis=-1, keepdims=True)
            acc = alpha * acc + _dot(p.astype(BF16), v)
            new.append((m_new, l, acc))
            g = jnp.max(reach[hh] - c_tile[:, :1] - m_new)
            gap = g if gap is None else jnp.maximum(gap, g)
        return tuple(new), gap

    init = (jnp.full((tq, 1), -jnp.inf, F32), jnp.zeros((tq, 1), F32), jnp.zeros((tq, PAIR), F32))
    carries, gap = tile(first, (init, init), True)

    def more(state):
        j, gap, _ = state
        return jnp.logical_and(j >= 0, gap >= EXP_ZERO - 1.0)

    def body(state):
        j, _, carries = state
        carries, gap = tile(j, carries, False)
        return j - 1, gap, carries

    _, _, carries = lax.while_loop(more, body, (first - 1, gap, carries))
    outs = [acc / l for _, l, acc in carries]
    o_ref[...] = _head_norm(jnp.where(_head_lanes((tq, PAIR), 0), outs[0], outs[1]))


def _pair_specs(S, tq, col_q, col_k, col_v):
    return [pl.BlockSpec((tq, PAIR), lambda p_, i: (i, col_q // PAIR + p_)),
            pl.BlockSpec((S, PAIR), lambda p_, i: (0, col_k // PAIR + p_)),
            pl.BlockSpec((S, PAIR), lambda p_, i: (0, col_v // PAIR + p_))]


def _fox(main, c, tq, tk):
    S = main.shape[0]
    return pl.pallas_call(
        functools.partial(_fox_kernel, tq=tq, tk=tk),
        grid=(2, S // tq),
        in_specs=_pair_specs(S, tq, COL_QC, COL_KC, COL_VC) + [pl.BlockSpec((1, 2, S), lambda p_, i: (p_, 0, 0))],
        out_specs=pl.BlockSpec((tq, PAIR), lambda p_, i: (i, p_)),
        out_shape=jax.ShapeDtypeStruct((S, GROUP_WIDTH), F32),
        scratch_shapes=[pltpu.VMEM((8, 128), F32)],
        compiler_params=_params("parallel", "arbitrary"),
        name="forgetting_attention",
    )(main, main, main, c)


def _sb_kernel(q_ref, k_ref, v_ref, tri_ref, o_ref, *, tq, tk):
    i = pl.program_id(1)
    per_q = tq // tk
    q = q_ref[...]
    tri = tri_ref[...]
    row = lax.broadcasted_iota(jnp.int32, (tq, tk), 0)
    col = lax.broadcasted_iota(jnp.int32, (tq, tk), 1)
    qs = [jnp.where(_head_lanes(q.shape, hh), q, jnp.zeros_like(q)) for hh in range(2)]

    def tile(j, carries, lead):
        ks = pl.multiple_of(j * tk, tk)
        k = k_ref[pl.ds(ks, tk), :]
        v = v_ref[pl.ds(ks, tk), :]
        strict = None if lead is None else col + lead * tk < row
        new, top = [], None
        for hh in range(2):
            run, acc = carries[hh]
            z = _dot_nt(qs[hh], k)
            log_keep = -_softplus(z)
            log_beta = z + log_keep
            if strict is not None:
                log_keep = jnp.where(strict, log_keep, 0.0)
            later = run + _dot3(log_keep, tri)
            a = jnp.exp(log_beta + later)
            if strict is not None:
                a = jnp.where(strict, a, 0.0)
            acc = acc + _dot(a.astype(BF16), v)
            run = run + jnp.sum(log_keep, axis=-1, keepdims=True)
            new.append((run, acc))
            t = jnp.max(run)
            top = t if top is None else jnp.maximum(top, t)
        return tuple(new), top

    init = (jnp.zeros((tq, 1), F32), jnp.zeros((tq, PAIR), F32))
    carries, top = (init, init), None
    for lead in reversed(range(per_q)):
        carries, top = tile(i * per_q + lead, carries, lead)

    def more(state):
        j, top, _ = state
        return jnp.logical_and(j >= 0, top >= EXP_ZERO)

    def body(state):
        j, _, carries = state
        carries, top = tile(j, carries, None)
        return j - 1, top, carries

    _, _, carries = lax.while_loop(more, body, (i * per_q - 1, top, carries))
    o_ref[...] = _head_norm(jnp.where(_head_lanes((tq, PAIR), 0), carries[0][1], carries[1][1]))


def _sb(main, tq, tk):
    S = main.shape[0]
    tri = (np.arange(tk)[:, None] > np.arange(tk)[None, :]).astype(np.float32)
    return pl.pallas_call(
        functools.partial(_sb_kernel, tq=tq, tk=tk),
        grid=(2, S // tq),
        in_specs=_pair_specs(S, tq, COL_QD, COL_KD, COL_VD) + [_resident((tk, tk), lambda p_, i: (0, 0))],
        out_specs=pl.BlockSpec((tq, PAIR), lambda p_, i: (i, p_)),
        out_shape=jax.ShapeDtypeStruct((S, GROUP_WIDTH), F32),
        compiler_params=_params("parallel", "parallel"),
        name="stick_breaking_attention",
    )(main, main, main, jnp.asarray(tri, BF16))


def _post_kernel(h_ref, oa_ref, ob_ref, oc_ref, od_ref, p_ref, ghead_ref, wout_ref, gmlp_ref, wup_ref,
                 wdown_ref, gple_ref, wgate_ref, bgate_ref, wproj_ref, gfinal_ref, out_ref, *, f_chunk, final):
    o = jnp.concatenate([oa_ref[...], ob_ref[...], oc_ref[...], od_ref[...]], axis=1)
    on = o * ghead_ref[...]
    h = h_ref[...] + _dot(on.astype(BF16), wout_ref[...])
    u = _rms(h, gmlp_ref[...]).astype(BF16)
    d_ff = wup_ref.shape[1]
    acc = jnp.zeros(h.shape, F32)
    for f in range(0, d_ff, f_chunk):
        hid = jnp.maximum(_dot(u, wup_ref[:, f:f + f_chunk]), 0.0)
        acc = acc + _dot((hid * hid).astype(BF16), wdown_ref[f:f + f_chunk, :])
    h = h + acc
    gate = jax.nn.sigmoid(_dot(_rms(h, gple_ref[...]).astype(BF16), wgate_ref[...]) + bgate_ref[...])
    h = h + _dot(p_ref[...].astype(BF16), wproj_ref[...]) * gate
    out_ref[...] = _rms(h, gfinal_ref[...]) if final else h


def _post(h, mixers, p, ghead, wout, gmlp, wup, wdown, gple, wgate, bgate, wproj, gfinal, tm, final):
    S, D = h.shape
    row = lambda i: (i, 0)
    fixed = lambda i: (0, 0)
    vec = _resident((1, D), fixed)
    return pl.pallas_call(
        functools.partial(_post_kernel, f_chunk=1024, final=final),
        grid=(S // tm,),
        in_specs=[pl.BlockSpec((tm, D), row)] + [pl.BlockSpec((tm, GROUP_WIDTH), row)] * 4
                 + [pl.BlockSpec((tm, p.shape[1]), row),
                    vec, _resident(wout.shape, fixed),
                    vec, _resident(wup.shape, fixed), _resident(wdown.shape, fixed),
                    vec, _resident(wgate.shape, fixed), vec, _resident(wproj.shape, fixed), vec],
        out_specs=pl.BlockSpec((tm, D), row),
        out_shape=jax.ShapeDtypeStruct((S, D), F32),
        compiler_params=_params("parallel"),
        name="outproj_mlp_ple",
    )(h, *mixers, p, ghead, wout, gmlp, wup, wdown, gple, wgate, bgate, wproj, gfinal)


def _inproj_columns():
    splits = (256, 64, 64, 64, 64, 64, 64, 12, 256, 256, 256, 256, 256, 256, 4, 256, 256, 256)
    offs = np.concatenate([[0], np.cumsum(splits)])
    seg = lambda k: np.arange(offs[k], offs[k + 1])
    main = [0, 1, 2, 3, 4, 5, 6, 8, 9, 10, 11, 12, 13, 15, 16, 17]
    perm = np.concatenate([seg(k) for k in main] + [seg(7), seg(14)])
    scale = np.ones(perm.shape[0], np.float32)
    for start in (COL_QA, COL_QB, COL_QC, COL_QD):
        scale[start:start + GROUP_WIDTH] = QK_SCALE
    return perm, scale


def _compress_weights(pos_k, w1_k, w2_k, pos_v, w1_v, w2_v):
    half = NSA_CMP_LEN // 2

    def expand_rows(wk, wv):
        z = jnp.zeros_like(wk).reshape(half, HEAD_DIM, -1)
        k_rows = jnp.concatenate([wk.reshape(half, HEAD_DIM, -1), z], axis=2)
        v_rows = jnp.concatenate([z, wv.reshape(half, HEAD_DIM, -1)], axis=2)
        return jnp.concatenate([k_rows, v_rows], axis=1).reshape(half * PAIR, -1)

    n_top = half * HEAD_DIM
    w1 = jnp.stack([expand_rows(w1_k[:n_top], w1_v[:n_top]), expand_rows(w1_k[n_top:], w1_v[n_top:])]).astype(BF16)
    pos = jnp.concatenate([pos_k, pos_v], axis=1)
    pos = jnp.stack([pos[:half].reshape(-1), pos[half:].reshape(-1)])
    zk = jnp.zeros_like(w2_k)
    w2 = jnp.concatenate([jnp.concatenate([w2_k, zk], axis=1), jnp.concatenate([zk, w2_v], axis=1)], axis=0)
    return pos, w1, w2.astype(BF16)


def _nsa_mixer(main, x16, gates, b_gate, cmp_weights, slopes, consts):
    S = main.shape[0]
    kvc = _compress(x16, *cmp_weights)
    tq_c = min(NSA_CQ, S)
    tq = min(NSA_TQ, S)
    tk = min(NSA_TK, S)
    o_cmp, selb, flags = _nsa_cmp(main, kvc, consts["cover"], consts["tile_of_block"], tq_c, slopes)
    n_kt = S // tk
    diag_tile = (np.arange(S // tq) * tq + tq - 1) // tk
    wanted = jnp.logical_and(flags[:, 0, :n_kt] > 0.0, np.arange(n_kt)[None, :] < diag_tile[:, None])
    tiles = jnp.argsort(jnp.logical_not(wanted), axis=1, stable=True).astype(jnp.int32).reshape(-1)
    counts = jnp.sum(wanted, axis=1, dtype=jnp.int32)
    rhs = jnp.concatenate([main[:, COL_KVSEL:COL_KVSEL + HEAD_DIM].T, consts["key_features"]], axis=0)
    bgate = jnp.pad(b_gate, (GATE_COL_NSA, N_GATE - GATE_COL_NSA - b_gate.shape[0])).reshape(1, N_GATE)
    return _nsa_attend(tiles, counts, main, rhs, selb, o_cmp, gates, bgate, tq, tk, slopes)


def _dilated_mixer(main, xd4, xd16, slopes):
    S = main.shape[0]
    main_cols = (COL_QB // PAIR, COL_KB // PAIR, COL_VB // PAIR)
    split_cols = (0, GROUP_WIDTH // PAIR, 2 * GROUP_WIDTH // PAIR)
    outs, lses = [], []
    for (window, d), src, cols in zip(DILATED_PATTERNS, (main[None], xd4, xd16), (main_cols, split_cols, split_cols)):
        assert src.shape[0] == d
        o, lse = _band(src, cols, d, window // d, slopes, min(BAND_TQ, S // d))
        outs.append(o)
        lses.append(lse)
    return _dil_combine(outs, lses, min(512, S))


def _fox_mixer(main, gates, b_f):
    S = main.shape[0]
    H = HEADS_PER_MIXER
    f = gates[:, GATE_COL_F:GATE_COL_F + H].T
    f = jnp.concatenate([f, jnp.zeros_like(f)], axis=0)
    b = jnp.concatenate([b_f, jnp.zeros_like(b_f)]).reshape(2 * H, 1)
    c = _logf_cumsum(f, b, min(512, S))[:H].reshape(2, 2, S)
    return _fox(main, c, min(FOX_TQ, S), min(FOX_TK, S))


def kernel(x, p, g_mix, w_in, b_nsa_gate, b_forget, cmp_pos_k, cmp_w1_k, cmp_w2_k, cmp_pos_v, cmp_w1_v, cmp_w2_v,
           g_head, w_out, g_mlp, w_up, w_down, g_ple, w_ple_gate, b_ple_gate, w_ple_proj, g_final):
    B, S, D = x.shape
    depth = w_in.shape[0]
    slopes = _alibi_slopes()
    slopes_nsa = tuple(float(s) for s in slopes[0::2])
    slopes_dil = tuple(float(s) for s in slopes[1::2])
    perm, colscale = _inproj_columns()
    n_pad = N_MAIN + N_GATE - perm.shape[0]

    n_blk = S // NSA_SEL_LEN
    n_cmp_pad = S // NSA_CMP_STRIDE
    cstart = np.arange(n_cmp_pad) * NSA_CMP_STRIDE
    bstart = np.arange(n_blk) * NSA_SEL_LEN
    cover = (cstart[:, None] < bstart[None, :] + NSA_SEL_LEN) & (cstart[:, None] + NSA_CMP_LEN - 1 >= bstart[None, :])
    tile_of_block = bstart[:, None] // min(NSA_TK, S) == np.arange(128)[None, :]
    pos = np.arange(S)
    key_features = np.zeros((HEAD_DIM + n_blk, S), np.float32)
    key_features[0] = pos // 128
    key_features[1] = pos % 128
    key_features[HEAD_DIM + pos // NSA_SEL_LEN, pos] = 1.0
    consts = {"cover": jnp.asarray(cover, BF16), "tile_of_block": jnp.asarray(tile_of_block, BF16),
              "key_features": jnp.asarray(key_features, BF16)}

    outs = []
    for b in range(B):
        h = x[b]
        for i in range(depth):
            w = jnp.pad(w_in[i][:, perm] * colscale, ((0, 0), (0, n_pad))).astype(BF16)
            main, gates, x16, xd4, xd16 = _inproj(h, g_mix[i].reshape(1, D), w, min(512, S))
            cmp_weights = _compress_weights(cmp_pos_k[i], cmp_w1_k[i], cmp_w2_k[i],
                                            cmp_pos_v[i], cmp_w1_v[i], cmp_w2_v[i])
            mixers = (_nsa_mixer(main, x16, gates, b_nsa_gate[i], cmp_weights, slopes_nsa, consts),
                      _dilated_mixer(main, xd4, xd16, slopes_dil),
                      _fox_mixer(main, gates, b_forget[i]),
                      _sb(main, min(SB_TQ, S), min(SB_TK, S)))
            h = _post(h, mixers, p[i, b], g_head[i].reshape(1, D), w_out[i].astype(BF16),
                      g_mlp[i].reshape(1, D), w_up[i].astype(BF16), w_down[i].astype(BF16),
                      g_ple[i].reshape(1, D), w_ple_gate[i].astype(BF16), b_ple_gate[i].reshape(1, D),
                      w_ple_proj[i].astype(BF16), g_final.reshape(1, D), min(256, S), i == depth - 1)
        outs.append(h)
    return jnp.stack(outs)
```
